```python
import jax
import jax.numpy as jnp
from jax import lax
import numpy as np

D_MODEL = 1024
BATCH = 4
SEQ = 4096
DEPTH = 4
DEC_BATCH = 32
DEC_SEQ = 4
PAST_LEN = 8192
PAGE_SIZE = 128

HEAD_DIM = 64
N_MEM = 256
H_MEM = 4
MEM_WIDTH = H_MEM * HEAD_DIM
SELF_WIDTH = D_MODEL - MEM_WIDTH
H_SELF = SELF_WIDTH // HEAD_DIM
N_RWKV = (DEPTH + 1) // 2
N_FOX = DEPTH // 2
DECAY_LORA = 64
AAA_LORA = 64
MV_LORA = 32
GATE_LORA = 128
D_FF = 2816
CONV_W = 3
Q_BLOCK = 128
NORM_EPS = 1e-6
GN_EPS = 64e-5
ATTN_SCALE = HEAD_DIM ** -0.5
FORGET_BIAS_MEAN = 4.0
CACHE_FORGET_LOGIT = 8.0

kernel_name = 'hybrid_rwkv7_fox_memxattn_convffn_step'


def _rms(x, g):
    xf = x.astype(jnp.float32)
    y = xf * lax.rsqrt(jnp.mean(xf * xf, axis=-1, keepdims=True) + NORM_EPS)
    return (y * g.astype(jnp.float32)).astype(x.dtype)


def _heads(t):
    return t.reshape(t.shape[:-1] + (H_SELF, HEAD_DIM))


def _rwkv_mixer(xn, shift_prev, wkv0, v_first, P, i):
    B, T, _ = xn.shape
    SW = SELF_WIDTH
    w_in, mix, vec = P['rw_w_in'][i], P['rw_mix'][i], P['rw_vecs'][i]
    prev = jnp.concatenate([shift_prev[:, None].astype(xn.dtype), xn[:, :-1]], axis=1)
    xx = prev - xn
    xr, xw, xk, xv, xa, xg = [xn + xx * mix[j] for j in range(6)]
    r = xr @ w_in[:, :SW]
    k = xk @ w_in[:, SW:2 * SW]
    v = xv @ w_in[:, 2 * SW:3 * SW]
    cq = xn @ w_in[:, 3 * SW:]
    w_pre = (vec[0] + jnp.tanh(xw @ P['rw_w1'][i]) @ P['rw_w2'][i]).astype(jnp.float32)
    decay = jnp.exp(-jnp.exp(-jax.nn.softplus(-w_pre) - 0.5))
    a = jax.nn.sigmoid((vec[1] + (xa @ P['rw_a1'][i]) @ P['rw_a2'][i]).astype(jnp.float32))
    if v_first is None:
        v_first = v
    else:
        j = i - 1
        nu = jax.nn.sigmoid(P['rw_v0'][j] + (xv @ P['rw_v1'][j]) @ P['rw_v2'][j])
        v = v + (v_first - v) * nu
    g = jax.nn.sigmoid(xg @ P['rw_g1'][i]) @ P['rw_g2'][i]
    kf = k.astype(jnp.float32)
    kk = _heads(kf * vec[2].astype(jnp.float32))
    kk = kk / jnp.maximum(jnp.sqrt(jnp.sum(kk * kk, axis=-1, keepdims=True)), 1e-12)
    kf = _heads(kf * (1.0 + (a - 1.0) * vec[3].astype(jnp.float32)))
    rf, vf, af, wf = [_heads(t.astype(jnp.float32)) for t in (r, v, a, decay)]

    def step(S, inp):
        r_t, w_t, k_t, v_t, kk_t, a_t = inp
        sa = jnp.einsum('bhij,bhj->bhi', S, -kk_t)
        S = (S * w_t[:, :, None, :] + sa[..., None] * (kk_t * a_t)[:, :, None, :]
             + v_t[..., None] * k_t[:, :, None, :])
        return S, jnp.einsum('bhij,bhj->bhi', S, r_t)

    xs = tuple(jnp.moveaxis(t, 1, 0) for t in (rf, wf, kf, vf, kk, af))
    S_fin, y = lax.scan(step, wkv0.astype(jnp.float32), xs)
    y = jnp.moveaxis(y, 0, 1)
    mu = jnp.mean(y, axis=-1, keepdims=True)
    var = jnp.mean(jnp.square(y - mu), axis=-1, keepdims=True)
    y = ((y - mu) * lax.rsqrt(var + GN_EPS)).reshape(B, T, SW)
    y = y * vec[5].astype(jnp.float32) + vec[6].astype(jnp.float32)
    r_k = vec[4].astype(jnp.float32).reshape(H_SELF, HEAD_DIM)
    bonus = jnp.sum(rf * kf * r_k, axis=-1, keepdims=True) * vf
    y = (y + bonus.reshape(B, T, SW)) * g.astype(jnp.float32)
    return y.astype(xn.dtype), cq, xn[:, -1], S_fin.astype(xn.dtype), v_first


def _fox_project(xn, P, i):
    SW = SELF_WIDTH
    h = xn @ P['fox_w_in'][i]
    q = _rms(_heads(h[..., :SW]), P['fox_qk_gain'][i, 0])
    k = _rms(_heads(h[..., SW:2 * SW]), P['fox_qk_gain'][i, 1])
    v = _heads(h[..., 2 * SW:3 * SW])
    o0 = 3 * SW + H_SELF
    logf = jax.nn.log_sigmoid((h[..., 3 * SW:o0] + P['fox_bf'][i]).astype(jnp.float32))
    og = jax.nn.sigmoid(h[..., o0:o0 + SW])
    cq = h[..., o0 + SW:]
    return q, k, v, logf, og, cq


def _fox_attend_prompt(q, k, v, logf):
    B, S, H, Dh = q.shape
    c = jnp.cumsum(logf, axis=1)
    cT = jnp.transpose(c, (0, 2, 1))
    kpos = jnp.arange(S)

    def block(bi):
        s0 = bi * Q_BLOCK
        qb = lax.dynamic_slice_in_dim(q, s0, Q_BLOCK, axis=1)
        cb = lax.dynamic_slice_in_dim(cT, s0, Q_BLOCK, axis=2)
        sc = jnp.einsum('bqhd,bkhd->bhqk', qb, k).astype(jnp.float32) * ATTN_SCALE
        sc = sc + cb[..., None] - cT[:, :, None, :]
        qpos = s0 + jnp.arange(Q_BLOCK)
        sc = jnp.where(kpos[None, :] <= qpos[:, None], sc, -jnp.inf)
        p = jax.nn.softmax(sc, axis=-1).astype(v.dtype)
        return jnp.einsum('bhqk,bkhd->bqhd', p, v)

    o = lax.map(block, jnp.arange(S // Q_BLOCK))
    return jnp.moveaxis(o, 0, 1).reshape(B, S, H * Dh)


def _fox_attend_sample(q, k_new, v_new, logf_new, kp, vp, lp):
    DB, T, H, Dh = q.shape
    kp = kp.reshape(DB, -1, H, Dh)
    vp = vp.reshape(DB, -1, H, Dh)
    lp = lp.reshape(DB, -1, H).astype(jnp.float32)
    P = kp.shape[1]
    c_past = lp - lax.cumsum(lp, axis=1, reverse=True)
    c_new = jnp.cumsum(logf_new, axis=1)
    cq = jnp.transpose(c_new, (0, 2, 1))[..., None]
    s_past = jnp.einsum('bqhd,bkhd->bhqk', q, kp.astype(q.dtype)).astype(jnp.float32) * ATTN_SCALE
    s_past = s_past + cq - jnp.transpose(c_past, (0, 2, 1))[:, :, None, :]
    s_new = jnp.einsum('bqhd,bkhd->bhqk', q, k_new).astype(jnp.float32) * ATTN_SCALE
    s_new = s_new + cq - jnp.transpose(c_new, (0, 2, 1))[:, :, None, :]
    causal = jnp.arange(T)[None, :] <= jnp.arange(T)[:, None]
    s_new = jnp.where(causal, s_new, -jnp.inf)
    p = jax.nn.softmax(jnp.concatenate([s_past, s_new], axis=-1), axis=-1).astype(v_new.dtype)
    o = (jnp.einsum('bhqk,bkhd->bqhd', p[..., :P], vp.astype(v_new.dtype))
         + jnp.einsum('bhqk,bkhd->bqhd', p[..., P:], v_new))
    return o.reshape(DB, T, H * Dh)


def _mem_kv(mem, gain, w_kv):
    B = mem.shape[0]
    k, v = jnp.split(_rms(mem, gain) @ w_kv, 2, axis=-1)
    return (k.reshape(B, N_MEM, H_MEM, HEAD_DIM), v.reshape(B, N_MEM, H_MEM, HEAD_DIM))


def _mem_attend(cq, mk, mv):
    B, T, _ = cq.shape
    q = cq.reshape(B, T, H_MEM, HEAD_DIM)
    s = jnp.einsum('bthd,bmhd->bhtm', q, mk.astype(q.dtype)).astype(jnp.float32) * ATTN_SCALE
    p = jax.nn.softmax(s, axis=-1).astype(q.dtype)
    return jnp.einsum('bhtm,bmhd->bthd', p, mv.astype(q.dtype)).reshape(B, T, MEM_WIDTH)


def _conv_ffn(xn, prev, P, l):
    T = xn.shape[1]
    u, g = jnp.split(xn @ P['ffn_w_up'][l], 2, axis=-1)
    full = jnp.concatenate([prev.astype(g.dtype), g], axis=1)
    w = P['ffn_conv_w'][l]
    gc = P['ffn_conv_b'][l] + full[:, 0:T] * w[0]
    for j in range(1, CONV_W):
        gc = gc + full[:, j:j + T] * w[j]
    return (jax.nn.gelu(gc) * u) @ P['ffn_w_down'][l], full[:, T:]


def _trunk(h, mem_k, mem_v, rw_shift, rw_wkv, ffn_conv, fox_attend, P):
    v_first = None
    shifts, wkvs, fks, fvs, fls, convs = [], [], [], [], [], []
    for l in range(DEPTH):
        gains = P['norm_gains'][l]
        xn = _rms(h, gains[0])
        i = l // 2
        if l % 2 == 0:
            o_self, cq, sh, S, v_first = _rwkv_mixer(xn, rw_shift[i], rw_wkv[i], v_first, P, i)
            shifts.append(sh)
            wkvs.append(S)
        else:
            q, k, v, logf, og, cq = _fox_project(xn, P, i)
            o_self = fox_attend(i, q, k, v, logf) * og
            fks.append(k)
            fvs.append(v)
            fls.append(logf.astype(h.dtype))
        o_mem = _mem_attend(cq, mem_k[l], mem_v[l])
        mixed = jnp.concatenate([o_self, o_mem], axis=-1) @ P['w_out'][l]
        h = h + _rms(mixed, gains[1])
        f, cs = _conv_ffn(_rms(h, gains[2]), ffn_conv[l], P, l)
        convs.append(cs)
        h = h + _rms(f, gains[3])
    return (h, jnp.stack(wkvs), jnp.stack(shifts), jnp.stack(fks), jnp.stack(fvs),
            jnp.stack(fls), jnp.stack(convs))


def setup_inputs(seed: int = 0) -> dict:
    key = jax.random.key(seed)
    keys = iter(jax.random.split(key, 48))

    def nrm(shape, scale=1.0, offset=0.0):
        return offset + scale * jax.random.normal(next(keys), shape, jnp.float32)

    n_pages = PAST_LEN // PAGE_SIZE
    n_used = DEC_BATCH * n_pages
    n_pool = n_used + n_used // 4
    sw, mw, d = SELF_WIDTH, MEM_WIDTH, D_MODEL
    page_table = jax.random.permutation(next(keys), n_pool)[:n_used].reshape(DEC_BATCH, n_pages).astype(jnp.int32)
    vec_off = jnp.array([0.0, 0.0, 0.85, 1.0, 0.0, 1.0, 0.0], jnp.float32)[None, :, None]
    vec_scale = jnp.array([0.5, 0.5, 0.1, 0.1, 0.5, 0.02, 0.02], jnp.float32)[None, :, None]
    return {
        'x_prompt': nrm((BATCH, SEQ, d)),
        'x_sample': nrm((DEC_BATCH, DEC_SEQ, d)),
        'cache_mem_k': nrm((DEPTH, DEC_BATCH, N_MEM, H_MEM, HEAD_DIM)),
        'cache_mem_v': nrm((DEPTH, DEC_BATCH, N_MEM, H_MEM, HEAD_DIM)),
        'cache_fox_k': nrm((N_FOX, n_pool, PAGE_SIZE, H_SELF, HEAD_DIM)),
        'cache_fox_v': nrm((N_FOX, n_pool, PAGE_SIZE, H_SELF, HEAD_DIM)),
        'cache_fox_logf': jax.nn.log_sigmoid(nrm((N_FOX, n_pool, PAGE_SIZE, H_SELF), 0.5, CACHE_FORGET_LOGIT)),
        'state_rwkv_wkv': nrm((N_RWKV, DEC_BATCH, H_SELF, HEAD_DIM, HEAD_DIM), 0.3),
        'state_rwkv_shift': nrm((N_RWKV, DEC_BATCH, d)),
        'state_ffn_conv': nrm((DEPTH, DEC_BATCH, CONV_W - 1, D_FF)),
        'page_table': page_table,
        'mem_prompt': nrm((BATCH, N_MEM, d)),
        'norm_gains': nrm((DEPTH, 4, d), 0.02, 1.0),
        'w_out': nrm((DEPTH, d, d), d ** -0.5),
        'mem_norm': nrm((DEPTH, d), 0.02, 1.0),
        'w_mem_kv': nrm((DEPTH, d, 2 * mw), d ** -0.5),
        'rw_mix': jax.random.uniform(next(keys), (N_RWKV, 6, d), jnp.float32),
        'rw_w_in': nrm((N_RWKV, d, 3 * sw + mw), d ** -0.5),
        'rw_vecs': vec_off + vec_scale * jax.random.normal(next(keys), (N_RWKV, 7, sw), jnp.float32),
        'rw_w1': nrm((N_RWKV, d, DECAY_LORA), d ** -0.5),
        'rw_w2': nrm((N_RWKV, DECAY_LORA, sw), DECAY_LORA ** -0.5),
        'rw_a1': nrm((N_RWKV, d, AAA_LORA), d ** -0.5),
        'rw_a2': nrm((N_RWKV, AAA_LORA, sw), AAA_LORA ** -0.5),
        'rw_g1': nrm((N_RWKV, d, GATE_LORA), d ** -0.5),
        'rw_g2': nrm((N_RWKV, GATE_LORA, sw), GATE_LORA ** -0.5),
        'rw_v0': nrm((N_RWKV - 1, sw), 0.5),
        'rw_v1': nrm((N_RWKV - 1, d, MV_LORA), d ** -0.5),
        'rw_v2': nrm((N_RWKV - 1, MV_LORA, sw), MV_LORA ** -0.5),
        'fox_w_in': nrm((N_FOX, d, 4 * sw + H_SELF + mw), d ** -0.5),
        'fox_bf': nrm((N_FOX, H_SELF), 0.1, FORGET_BIAS_MEAN),
        'fox_qk_gain': nrm((N_FOX, 2, HEAD_DIM), 0.02, 1.0),
        'ffn_w_up': nrm((DEPTH, d, 2 * D_FF), d ** -0.5),
        'ffn_conv_w': nrm((DEPTH, CONV_W, D_FF), CONV_W ** -0.5),
        'ffn_conv_b': nrm((DEPTH, D_FF), 0.02),
        'ffn_w_down': nrm((DEPTH, D_FF, d), D_FF ** -0.5),
    }


def reference(x_prompt, x_sample, cache_mem_k, cache_mem_v, cache_fox_k, cache_fox_v, cache_fox_logf,
              state_rwkv_wkv, state_rwkv_shift, state_ffn_conv, page_table, mem_prompt,
              norm_gains, w_out, mem_norm, w_mem_kv, rw_mix, rw_w_in, rw_vecs, rw_w1, rw_w2,
              rw_a1, rw_a2, rw_g1, rw_g2, rw_v0, rw_v1, rw_v2, fox_w_in, fox_bf, fox_qk_gain,
              ffn_w_up, ffn_conv_w, ffn_conv_b, ffn_w_down):
    P = dict(norm_gains=norm_gains, w_out=w_out, rw_mix=rw_mix, rw_w_in=rw_w_in, rw_vecs=rw_vecs,
             rw_w1=rw_w1, rw_w2=rw_w2, rw_a1=rw_a1, rw_a2=rw_a2, rw_g1=rw_g1, rw_g2=rw_g2,
             rw_v0=rw_v0, rw_v1=rw_v1, rw_v2=rw_v2, fox_w_in=fox_w_in, fox_bf=fox_bf,
             fox_qk_gain=fox_qk_gain, ffn_w_up=ffn_w_up, ffn_conv_w=ffn_conv_w,
             ffn_conv_b=ffn_conv_b, ffn_w_down=ffn_w_down)

    mks, mvs = [], []
    for l in range(DEPTH):
        mk, mv = _mem_kv(mem_prompt, mem_norm[l], w_mem_kv[l])
        mks.append(mk)
        mvs.append(mv)
    p_mem_k = jnp.stack(mks)
    p_mem_v = jnp.stack(mvs)
    B = x_prompt.shape[0]
    dt = x_prompt.dtype
    shift0 = jnp.zeros((N_RWKV, B, D_MODEL), dt)
    wkv0 = jnp.zeros((N_RWKV, B, H_SELF, HEAD_DIM, HEAD_DIM), dt)
    conv0 = jnp.zeros((DEPTH, B, CONV_W - 1, D_FF), dt)

    def fox_prompt(i, q, k, v, logf):
        return _fox_attend_prompt(q, k, v, logf)

    (y_prompt, p_rwkv_wkv, p_rwkv_shift, p_fox_k, p_fox_v, p_fox_logf,
     p_ffn_conv) = _trunk(x_prompt, p_mem_k, p_mem_v, shift0, wkv0, conv0, fox_prompt, P)

    def fox_sample(i, q, k, v, logf):
        return _fox_attend_sample(q, k, v, logf, cache_fox_k[i][page_table],
                                  cache_fox_v[i][page_table], cache_fox_logf[i][page_table])

    (y_sample, s_rwkv_wkv, s_rwkv_shift, s_fox_k, s_fox_v, s_fox_logf,
     s_ffn_conv) = _trunk(x_sample, cache_mem_k, cache_mem_v, state_rwkv_shift, state_rwkv_wkv,
                          state_ffn_conv, fox_sample, P)

    return (y_prompt, y_sample, p_rwkv_wkv, p_rwkv_shift, p_fox_k, p_fox_v, p_fox_logf,
            p_mem_k, p_mem_v, p_ffn_conv, s_rwkv_wkv, s_rwkv_shift, s_fox_k, s_fox_v,
            s_fox_logf, s_ffn_conv)
```

```python
import functools

import jax
import jax.numpy as jnp
from jax import lax
from jax.experimental import pallas as pl
from jax.experimental.pallas import tpu as pltpu

F32 = jnp.float32
BF16 = jnp.bfloat16

HEAD_DIM = 64
N_MEM = 256
H_MEM = 4
MEM_WIDTH = H_MEM * HEAD_DIM
NORM_EPS = 1e-6
GN_EPS = 64e-5
ATTN_SCALE = HEAD_DIM ** -0.5
CONV_W = 3
LANES = 128
LORA_PAD = 128
VMEM_LIMIT = 56 * 1024 * 1024
NEG_BIG = -1e30
WKV_CHUNK = 64


def _params(n_axes):
    return pltpu.CompilerParams(dimension_semantics=("arbitrary",) * n_axes, vmem_limit_bytes=VMEM_LIMIT)


def _rms(x, g):
    return x * lax.rsqrt(jnp.mean(x * x, axis=-1, keepdims=True) + NORM_EPS) * g


def _dot(a, b):
    return jnp.dot(a.astype(BF16), b.astype(BF16), preferred_element_type=F32)


def _dot_nt(a, b):
    return lax.dot_general(a.astype(BF16), b.astype(BF16), (((1,), (1,)), ((), ())), preferred_element_type=F32)


def _dot_tn(a, b):
    return lax.dot_general(a.astype(BF16), b.astype(BF16), (((0,), (0,)), ((), ())), preferred_element_type=F32)


def _sigmoid(x):
    return 1.0 / (1.0 + jnp.exp(-x))


def _softplus(x):
    return jnp.maximum(x, 0.0) + jnp.log(1.0 + jnp.exp(-jnp.abs(x)))


def _shift_rows(x, carry, shift):
    n = x.shape[0]
    c = carry.shape[0]
    if shift % 8 == 0:
        return jnp.concatenate([carry[c - shift:], x[:n - shift]], axis=0)
    row = lax.broadcasted_iota(jnp.int32, x.shape, 0)
    out = pltpu.roll(x, shift, axis=0)
    for j in range(shift):
        out = jnp.where(row == j, carry[c - shift + j:c - shift + j + 1], out)
    return out


def _cumsum_rows(x):
    n = x.shape[0]
    row = lax.broadcasted_iota(jnp.int32, x.shape, 0)
    s = 1
    while s < n:
        x = x + jnp.where(row >= s, pltpu.roll(x, s, axis=0), 0.0)
        s *= 2
    return x


def _const_spec(shape):
    nd = len(shape)
    return pl.BlockSpec(shape, lambda *_: (0,) * nd)


def _memkv_kernel(mem_ref, gain_ref, w_ref, k_ref, v_ref):
    xn = _rms(mem_ref[0], gain_ref[0])
    kv = _dot(xn, w_ref[0])
    k_ref[0, 0] = kv[:, :MEM_WIDTH]
    v_ref[0, 0] = kv[:, MEM_WIDTH:]


def _mem_kv(mem, mem_norm, w_mem_kv):
    depth, d = mem_norm.shape
    b = mem.shape[0]
    out = jax.ShapeDtypeStruct((depth, b, N_MEM, MEM_WIDTH), F32)
    return pl.pallas_call(
        _memkv_kernel,
        grid=(depth, b),
        in_specs=[pl.BlockSpec((1, N_MEM, d), lambda l, i: (i, 0, 0)),
                  pl.BlockSpec((1, 1, d), lambda l, i: (l, 0, 0)),
                  pl.BlockSpec((1, d, 2 * MEM_WIDTH), lambda l, i: (l, 0, 0))],
        out_specs=[pl.BlockSpec((1, 1, N_MEM, MEM_WIDTH), lambda l, i: (l, i, 0, 0))] * 2,
        out_shape=[out, out],
        compiler_params=_params(2),
        name="mem_kv",
    )(mem, mem_norm.reshape(depth, 1, d), w_mem_kv.astype(BF16))


def _rwkv_in_kernel(*refs, sw, shift, blocks_per_seq, has_vfirst):
    if has_vfirst:
        (h_ref, init_ref, vfirst_ref, gain_ref, mix_ref, w_in_ref, w1_ref, w2_ref, a1_ref, a2_ref, g1_ref, g2_ref,
         v1_ref, v2_ref, v0_ref, vecs_ref,
         r_ref, lw_ref, k_ref, v_ref, kk_ref, a_ref, g_ref, cq_ref, shift_ref, carry_ref) = refs
    else:
        (h_ref, init_ref, gain_ref, mix_ref, w_in_ref, w1_ref, w2_ref, a1_ref, a2_ref, g1_ref, g2_ref, vecs_ref,
         r_ref, lw_ref, k_ref, v_ref, kk_ref, a_ref, g_ref, cq_ref, shift_ref, carry_ref) = refs

    @pl.when(pl.program_id(0) % blocks_per_seq == 0)
    def _():
        carry_ref[...] = init_ref[0]

    xn = _rms(h_ref[...], gain_ref[...])
    tm = xn.shape[0]
    prev = _shift_rows(xn, carry_ref[...], shift)
    carry_ref[...] = xn[tm - shift:]
    shift_ref[0] = xn[tm - shift:]
    xx = prev - xn
    xr, xw, xk, xv, xa, xg = [(xn + xx * mix_ref[j:j + 1]).astype(BF16) for j in range(6)]
    r = _dot(xr, w_in_ref[:, 0:sw])
    k = _dot(xk, w_in_ref[:, sw:2 * sw])
    v = _dot(xv, w_in_ref[:, 2 * sw:3 * sw])
    cq_ref[...] = _dot(xn, w_in_ref[:, 3 * sw:]).astype(BF16)
    w_pre = vecs_ref[0:1] + _dot(jnp.tanh(_dot(xw, w1_ref[...])), w2_ref[...])
    lw_ref[...] = -jnp.exp(-_softplus(-w_pre) - 0.5)
    a = _sigmoid(vecs_ref[1:2] + _dot(_dot(xa, a1_ref[...]), a2_ref[...]))
    if has_vfirst:
        nu = _sigmoid(v0_ref[...] + _dot(_dot(xv, v1_ref[...]), v2_ref[...]))
        v = v + (vfirst_ref[...] - v) * nu
    g_ref[...] = _dot(_sigmoid(_dot(xg, g1_ref[...])), g2_ref[...])
    r_ref[...] = r
    v_ref[...] = v
    a_ref[...] = a
    kk_ref[...] = k * vecs_ref[2:3]
    k_ref[...] = k * (1.0 + (a - 1.0) * vecs_ref[3:4])


def _pad_cols(w, n):
    return jnp.pad(w, ((0, 0), (0, n - w.shape[1])))


def _pad_rows(w, n):
    return jnp.pad(w, ((0, n - w.shape[0]), (0, 0)))


def _rwkv_in(h, shift_init, v_first, gain, mix, w_in, w1, w2, a1, a2, g1, g2, v1, v2, v0, vecs, *, tm, shift,
             blocks_per_seq):
    n, d = h.shape
    sw = vecs.shape[1]
    n_seq = shift_init.shape[0]
    has_vfirst = v_first is not None
    row = lambda i: (i, 0)
    seq = lambda i: (i // blocks_per_seq, 0, 0)
    lora = lambda wa, wb: (_pad_cols(wa, LORA_PAD).astype(BF16), _pad_rows(wb, LORA_PAD).astype(BF16))
    w1p, w2p = lora(w1, w2)
    a1p, a2p = lora(a1, a2)
    g1p, g2p = lora(g1, g2)
    ins = [h, shift_init]
    specs = [pl.BlockSpec((tm, d), row), pl.BlockSpec((1, shift, d), seq)]
    if has_vfirst:
        ins.append(v_first)
        specs.append(pl.BlockSpec((tm, sw), row))
    consts = [gain.reshape(1, d), _pad_rows(mix, 8), w_in.astype(BF16), w1p, w2p, a1p, a2p, g1p, g2p]
    if has_vfirst:
        v1p, v2p = lora(v1, v2)
        consts += [v1p, v2p, v0.reshape(1, sw)]
    consts.append(_pad_rows(vecs, 8))
    ins += consts
    specs += [_const_spec(c.shape) for c in consts]
    wide = jax.ShapeDtypeStruct((n, sw), F32)
    outs = [wide] * 7 + [jax.ShapeDtypeStruct((n, MEM_WIDTH), BF16), jax.ShapeDtypeStruct((n_seq, shift, d), F32)]
    out_specs = [pl.BlockSpec((tm, sw), row)] * 7 + [pl.BlockSpec((tm, MEM_WIDTH), row),
                                                     pl.BlockSpec((1, shift, d), seq)]
    return pl.pallas_call(
        functools.partial(_rwkv_in_kernel, sw=sw, shift=shift, blocks_per_seq=blocks_per_seq, has_vfirst=has_vfirst),
        grid=(n // tm,),
        in_specs=specs, out_specs=out_specs, out_shape=outs,
        scratch_shapes=[pltpu.VMEM((shift, d), F32)],
        compiler_params=_params(1),
        name="rwkv_in",
    )(*ins)


def _wkv_kernel(*refs, chunk, n_chunks, has_init):
    if has_init:
        (r_ref, lw_ref, k_ref, v_ref, kk_ref, a_ref, g_ref, vecs_ref, s0_ref, o_ref, sfin_ref, s_scr) = refs
    else:
        (r_ref, lw_ref, k_ref, v_ref, kk_ref, a_ref, g_ref, vecs_ref, o_ref, sfin_ref, s_scr) = refs
    tb = pl.program_id(2)
    c = chunk
    hd = HEAD_DIM

    @pl.when(tb == 0)
    def _():
        if has_init:
            s_scr[...] = s0_ref[0]
        else:
            s_scr[...] = jnp.zeros_like(s_scr)

    ri = lax.broadcasted_iota(jnp.int32, (c, c), 0)
    ci = lax.broadcasted_iota(jnp.int32, (c, c), 1)
    tri_incl = (ci <= ri).astype(F32)
    tri_strict = (ci < ri).astype(F32)
    eye = (ci == ri).astype(F32)
    n_double = max(c.bit_length() - 2, 0)

    for ch in range(n_chunks):
        rows = slice(ch * c, (ch + 1) * c)
        lw = lw_ref[rows, :]
        cs = _cumsum_rows(lw)
        g_inc = jnp.exp(cs)
        g_exc = jnp.exp(cs - lw)
        g_inv = jnp.exp(-cs)
        r_all, k_all, v_all = r_ref[rows, :], k_ref[rows, :], v_ref[rows, :]
        kk_all, a_all, gate_all = kk_ref[rows, :], a_ref[rows, :], g_ref[rows, :]
        outs = []
        for h in range(2):
            sl = slice(h * hd, (h + 1) * hd)
            r, k, v, kk, a = r_all[:, sl], k_all[:, sl], v_all[:, sl], kk_all[:, sl], a_all[:, sl]
            kk = kk / jnp.maximum(jnp.sqrt(jnp.sum(kk * kk, axis=-1, keepdims=True)), 1e-12)
            rt = r * g_inc[:, sl]
            at = -kk * g_exc[:, sl]
            bt = kk * a * g_inv[:, sl]
            kt = k * g_inv[:, sl]
            gram = _dot_nt(jnp.concatenate([rt, at], axis=0), jnp.concatenate([bt, kt], axis=0))
            m_rb = gram[:c, :c] * tri_incl
            m_rk = gram[:c, c:] * tri_incl
            m_ab = gram[c:, :c] * tri_strict
            m_ak = gram[c:, c:] * tri_strict
            inv = eye + m_ab
            p = m_ab
            for _ in range(n_double):
                p = _dot(p, p)
                inv = inv + _dot(inv, p)
            s = s_scr[h]
            sa = _dot(inv, _dot_nt(at, s) + _dot(m_ak, v))
            y = _dot_nt(rt, s) + _dot(m_rb, sa) + _dot(m_rk, v)
            s_scr[h] = (s + _dot_tn(sa, bt) + _dot_tn(v, kt)) * g_inc[c - 1:c, sl]
            mu = jnp.mean(y, axis=-1, keepdims=True)
            var = jnp.mean(jnp.square(y - mu), axis=-1, keepdims=True)
            yn = (y - mu) * lax.rsqrt(var + GN_EPS) * vecs_ref[5:6, sl] + vecs_ref[6:7, sl]
            bonus = jnp.sum(r * k * vecs_ref[4:5, sl], axis=-1, keepdims=True) * v
            outs.append((yn + bonus) * gate_all[:, sl])
        o_ref[rows, :] = jnp.concatenate(outs, axis=-1)

    @pl.when(tb == pl.num_programs(2) - 1)
    def _():
        sfin_ref[0] = s_scr[...]


def _wkv(r, lw, k, v, kk, a, g, vecs, s0, *, n_seq, seq_len, chunk, n_chunks):
    n, sw = r.shape
    n_heads = sw // HEAD_DIM
    tb = chunk * n_chunks
    nb = seq_len // tb
    blk = pl.BlockSpec((tb, LANES), lambda b, hp, t: (b * nb + t, hp))
    ins = [r, lw, k, v, kk, a, g, _pad_rows(vecs, 8)]
    specs = [blk] * 7 + [pl.BlockSpec((8, LANES), lambda b, hp, t: (0, hp))]
    state_spec = pl.BlockSpec((1, 2, HEAD_DIM, HEAD_DIM), lambda b, hp, t: (b, hp, 0, 0))
    if s0 is not None:
        ins.append(s0)
        specs.append(state_spec)
    return pl.pallas_call(
        functools.partial(_wkv_kernel, chunk=chunk, n_chunks=n_chunks, has_init=s0 is not None),
        grid=(n_seq, n_heads // 2, nb),
        in_specs=specs,
        out_specs=[blk, state_spec],
        out_shape=[jax.ShapeDtypeStruct((n, sw), F32),
                   jax.ShapeDtypeStruct((n_seq, n_heads, HEAD_DIM, HEAD_DIM), F32)],
        scratch_shapes=[pltpu.VMEM((2, HEAD_DIM, HEAD_DIM), F32)],
        compiler_params=_params(3),
        name="wkv",
    )(*ins)


def _mem_attn_kernel(q_ref, k_ref, v_ref, o_ref):
    q = q_ref[0]
    k = k_ref[0].astype(BF16)
    v = v_ref[0].astype(BF16)
    outs = []
    for h in range(H_MEM):
        sl = slice(h * HEAD_DIM, (h + 1) * HEAD_DIM)
        s = _dot_nt(q[:, sl], k[:, sl]) * ATTN_SCALE
        p = jnp.exp(s - jnp.max(s, axis=-1, keepdims=True))
        p = p / jnp.sum(p, axis=-1, keepdims=True)
        outs.append(_dot(p, v[:, sl]))
    o_ref[0] = jnp.concatenate(outs, axis=-1).astype(BF16)


def _mem_attn(cq, mem_k, mem_v, *, tq):
    n_seq, t, _ = cq.shape
    qspec = pl.BlockSpec((1, tq, MEM_WIDTH), lambda b, i: (b, i, 0))
    mspec = pl.BlockSpec((1, N_MEM, MEM_WIDTH), lambda b, i: (b, 0, 0))
    return pl.pallas_call(
        _mem_attn_kernel,
        grid=(n_seq, t // tq),
        in_specs=[qspec, mspec, mspec],
        out_specs=qspec,
        out_shape=jax.ShapeDtypeStruct(cq.shape, BF16),
        compiler_params=_params(2),
        name="mem_attn",
    )(cq, mem_k, mem_v)


def _out_proj_kernel(os_ref, om_ref, h_ref, w_ref, gain_ref, o_ref, *, sw):
    mixed = _dot(os_ref[...], w_ref[0:sw, :]) + _dot(om_ref[...], w_ref[sw:, :])
    o_ref[...] = h_ref[...] + _rms(mixed, gain_ref[...])


def _out_proj(o_self, o_mem, h, w_out, gain, *, tm):
    n, d = h.shape
    sw = o_self.shape[1]
    row = lambda i: (i, 0)
    return pl.pallas_call(
        functools.partial(_out_proj_kernel, sw=sw),
        grid=(n // tm,),
        in_specs=[pl.BlockSpec((tm, sw), row), pl.BlockSpec((tm, MEM_WIDTH), row), pl.BlockSpec((tm, d), row),
                  _const_spec(w_out.shape), _const_spec((1, d))],
        out_specs=pl.BlockSpec((tm, d), row),
        out_shape=jax.ShapeDtypeStruct((n, d), F32),
        compiler_params=_params(1),
        name="out_proj",
    )(o_self, o_mem, h, w_out.astype(BF16), gain.reshape(1, d))


def _gelu_tanh(x):
    return 0.5 * x * (1.0 + jnp.tanh(0.7978845608028654 * (x + 0.044715 * (x * x * x))))


def _ffn_kernel(h_ref, init_ref, gin_ref, gout_ref, wup_ref, cw_ref, cb_ref, wdn_ref, o_ref, conv_ref, carry_ref,
                acc_ref, *, d_ff, shift, blocks_per_seq, ff_chunk):
    @pl.when(pl.program_id(0) % blocks_per_seq == 0)
    def _():
        carry_ref[...] = init_ref[0]

    h = h_ref[...]
    tm = h.shape[0]
    xn = _rms(h, gin_ref[...]).astype(BF16)
    for j in range(d_ff // ff_chunk):
        cols = slice(j * ff_chunk, (j + 1) * ff_chunk)
        u = _dot(xn, wup_ref[:, cols])
        g = _dot(xn, wup_ref[:, d_ff + j * ff_chunk:d_ff + (j + 1) * ff_chunk])
        carry = carry_ref[:, cols]
        g1 = _shift_rows(g, carry, shift)
        g2 = _shift_rows(g1, carry[:shift], shift)
        gc = cb_ref[:, cols] + g2 * cw_ref[0:1, cols] + g1 * cw_ref[1:2, cols] + g * cw_ref[2:3, cols]
        carry_ref[:, cols] = g[tm - 2 * shift:]
        conv_ref[0, :, cols] = g[tm - 2 * shift:]
        part = _dot(_gelu_tanh(gc) * u, wdn_ref[cols, :])
        if j == 0:
            acc_ref[...] = part
        else:
            acc_ref[...] += part
    o_ref[...] = h + _rms(acc_ref[...], gout_ref[...])


def _ffn(h, conv_init, gain_in, gain_out, w_up, conv_w, conv_b, w_down, *, tm, shift, blocks_per_seq):
    n, d = h.shape
    d_ff = w_down.shape[0]
    n_seq = conv_init.shape[0]
    row = lambda i: (i, 0)
    seq = lambda i: (i // blocks_per_seq, 0, 0)
    once = pl.Buffered(1)
    return pl.pallas_call(
        functools.partial(_ffn_kernel, d_ff=d_ff, shift=shift, blocks_per_seq=blocks_per_seq, ff_chunk=2 * LANES),
        grid=(n // tm,),
        in_specs=[pl.BlockSpec((tm, d), row), pl.BlockSpec((1, 2 * shift, d_ff), seq),
                  _const_spec((1, d)), _const_spec((1, d)),
                  pl.BlockSpec(w_up.shape, lambda i: (0, 0), pipeline_mode=once),
                  _const_spec((8, d_ff)), _const_spec((1, d_ff)),
                  pl.BlockSpec(w_down.shape, lambda i: (0, 0), pipeline_mode=once)],
        out_specs=[pl.BlockSpec((tm, d), row), pl.BlockSpec((1, 2 * shift, d_ff), seq)],
        out_shape=[jax.ShapeDtypeStruct((n, d), F32), jax.ShapeDtypeStruct((n_seq, 2 * shift, d_ff), F32)],
        scratch_shapes=[pltpu.VMEM((2 * shift, d_ff), F32), pltpu.VMEM((tm, d), F32)],
        compiler_params=_params(1),
        name="ffn",
    )(h, conv_init, gain_in.reshape(1, d), gain_out.reshape(1, d), w_up.astype(BF16), _pad_rows(conv_w, 8),
      conv_b.reshape(1, d_ff), w_down.astype(BF16))


def _head_mean_sq(x, ones_bd):
    outs = []
    for j in range(x.shape[1] // 256):
        sq = jnp.square(x[:, j * 256:(j + 1) * 256])
        hi = sq.astype(BF16)
        lo = (sq - hi.astype(F32)).astype(BF16)
        outs.append(jnp.dot(hi, ones_bd, preferred_element_type=F32) + jnp.dot(lo, ones_bd, preferred_element_type=F32))
    return jnp.concatenate(outs, axis=-1) * (1.0 / HEAD_DIM)


def _fox_in_kernel(h_ref, gain_ref, w_ref, bf_ref, qkg_ref, q_ref, k_ref, v_ref, lf_ref, c_ref, ct_ref, og_ref, cq_ref,
                   carry_ref, *, sw, shift, steps, blocks_per_seq):
    xn = _rms(h_ref[...], gain_ref[...]).astype(BF16)
    tm = xn.shape[0]
    ri = lax.broadcasted_iota(jnp.int32, (256, 256), 0) // HEAD_DIM
    ci = lax.broadcasted_iota(jnp.int32, (256, 256), 1) // HEAD_DIM
    ones_bd = (ri == ci).astype(BF16)
    q = jnp.dot(xn, w_ref[:, 0:sw], preferred_element_type=F32)
    q = q * lax.rsqrt(_head_mean_sq(q, ones_bd) + NORM_EPS) * qkg_ref[0:1]
    q_ref[...] = (q * ATTN_SCALE).astype(BF16)
    k = jnp.dot(xn, w_ref[:, sw:2 * sw], preferred_element_type=F32)
    k_ref[...] = k * lax.rsqrt(_head_mean_sq(k, ones_bd) + NORM_EPS) * qkg_ref[1:2]
    v_ref[...] = jnp.dot(xn, w_ref[:, 2 * sw:3 * sw], preferred_element_type=F32)
    o0 = 3 * sw + LANES
    lf = -_softplus(-(jnp.dot(xn, w_ref[:, 3 * sw:o0], preferred_element_type=F32) + bf_ref[...]))
    lf_ref[...] = lf
    og_ref[...] = _sigmoid(jnp.dot(xn, w_ref[:, o0:o0 + sw], preferred_element_type=F32))
    cq_ref[...] = jnp.dot(xn, w_ref[:, o0 + sw:], preferred_element_type=F32).astype(BF16)
    if steps is None:
        @pl.when(pl.program_id(0) % blocks_per_seq == 0)
        def _():
            carry_ref[...] = jnp.zeros_like(carry_ref)

        c = _cumsum_rows(lf) + carry_ref[0:1]
        carry_ref[...] = jnp.broadcast_to(c[tm - 1:tm], carry_ref.shape)
    else:
        parts = [lf[0:shift]]
        for t in range(1, steps):
            parts.append(parts[-1] + lf[t * shift:(t + 1) * shift])
        c = jnp.concatenate(parts, axis=0)
    c_ref[...] = c
    ct_ref[0] = c.T


def _fox_in(h, gain, w_in, bf, qk_gain, *, tm, shift, steps, blocks_per_seq):
    n, d = h.shape
    n_heads = bf.shape[0]
    sw = n_heads * HEAD_DIM
    n_seq = n // (tm * blocks_per_seq)
    seq_len = tm * blocks_per_seq
    w = jnp.concatenate([w_in[:, :3 * sw], _pad_cols(w_in[:, 3 * sw:3 * sw + n_heads], LANES),
                         w_in[:, 3 * sw + n_heads:]], axis=1).astype(BF16)
    qkg = _pad_rows(jnp.tile(qk_gain, (1, n_heads)), 8)
    row = lambda i: (i, 0)
    wide = pl.BlockSpec((tm, sw), row)
    lane = pl.BlockSpec((tm, LANES), row)
    return pl.pallas_call(
        functools.partial(_fox_in_kernel, sw=sw, shift=shift, steps=steps, blocks_per_seq=blocks_per_seq),
        grid=(n // tm,),
        in_specs=[pl.BlockSpec((tm, d), row), _const_spec((1, d)), _const_spec(w.shape), _const_spec((1, LANES)),
                  _const_spec((8, sw))],
        out_specs=[wide, wide, wide, lane, lane,
                   pl.BlockSpec((1, LANES, tm), lambda i: (i // blocks_per_seq, 0, i % blocks_per_seq)),
                   wide, pl.BlockSpec((tm, MEM_WIDTH), row)],
        out_shape=[jax.ShapeDtypeStruct((n, sw), BF16), jax.ShapeDtypeStruct((n, sw), F32),
                   jax.ShapeDtypeStruct((n, sw), F32), jax.ShapeDtypeStruct((n, LANES), F32),
                   jax.ShapeDtypeStruct((n, LANES), F32), jax.ShapeDtypeStruct((n_seq, LANES, seq_len), F32),
                   jax.ShapeDtypeStruct((n, sw), F32), jax.ShapeDtypeStruct((n, MEM_WIDTH), BF16)],
        scratch_shapes=[pltpu.VMEM((8, LANES), F32)],
        compiler_params=_params(1),
        name="fox_in",
    )(h, gain.reshape(1, d), w, _pad_cols(bf.reshape(1, n_heads), LANES), qkg)


def _fox_flash_kernel(q_ref, k_ref, v_ref, c_ref, ct_ref, og_ref, o_ref, m_ref, l_ref, acc_ref, *, n_heads, tq, tk):
    qi = pl.program_id(1)
    ki = pl.program_id(2)
    last = ((qi + 1) * tq - 1) // tk

    @pl.when(ki == 0)
    def _():
        m_ref[...] = jnp.full_like(m_ref, NEG_BIG)
        l_ref[...] = jnp.zeros_like(l_ref)
        acc_ref[...] = jnp.zeros_like(acc_ref)

    @pl.when(ki <= last)
    def _():
        qpos = qi * tq + lax.broadcasted_iota(jnp.int32, (tq, tk), 0)
        kpos = ki * tk + lax.broadcasted_iota(jnp.int32, (tq, tk), 1)
        causal = kpos <= qpos
        for h in range(n_heads):
            sl = slice(h * HEAD_DIM, (h + 1) * HEAD_DIM)
            s = _dot_nt(q_ref[:, sl], k_ref[:, sl])
            s = s + c_ref[:, h:h + 1] - ct_ref[0, h:h + 1, :]
            s = jnp.where(causal, s, NEG_BIG)
            m_old = m_ref[:, h:h + 1]
            m_new = jnp.maximum(m_old, jnp.max(s, axis=-1, keepdims=True))
            alpha = jnp.exp(m_old - m_new)
            p = jnp.exp(s - m_new)
            l_ref[:, h:h + 1] = alpha * l_ref[:, h:h + 1] + jnp.sum(p, axis=-1, keepdims=True)
            acc_ref[:, sl] = alpha * acc_ref[:, sl] + _dot(p, v_ref[:, sl])
            m_ref[:, h:h + 1] = m_new

    @pl.when(ki == last)
    def _():
        outs = []
        for h in range(n_heads):
            sl = slice(h * HEAD_DIM, (h + 1) * HEAD_DIM)
            outs.append(acc_ref[:, sl] / l_ref[:, h:h + 1])
        o_ref[...] = jnp.concatenate(outs, axis=-1) * og_ref[...]


def _fox_flash(q, k, v, c, ct, og, *, n_seq, seq_len, tq, tk):
    n, sw = k.shape
    n_heads = sw // HEAD_DIM
    nq, nk = seq_len // tq, seq_len // tk
    qrow = lambda b, i, j: (b * nq + i, 0)
    krow = lambda b, i, j: (b * nk + jnp.minimum(j, ((i + 1) * tq - 1) // tk), 0)
    return pl.pallas_call(
        functools.partial(_fox_flash_kernel, n_heads=n_heads, tq=tq, tk=tk),
        grid=(n_seq, nq, nk),
        in_specs=[pl.BlockSpec((tq, sw), qrow), pl.BlockSpec((tk, sw), krow), pl.BlockSpec((tk, sw), krow),
                  pl.BlockSpec((tq, LANES), qrow),
                  pl.BlockSpec((1, LANES, tk), lambda b, i, j: (b, 0, jnp.minimum(j, ((i + 1) * tq - 1) // tk))),
                  pl.BlockSpec((tq, sw), qrow)],
        out_specs=pl.BlockSpec((tq, sw), qrow),
        out_shape=jax.ShapeDtypeStruct((n, sw), F32),
        scratch_shapes=[pltpu.VMEM((tq, LANES), F32), pltpu.VMEM((tq, LANES), F32), pltpu.VMEM((tq, sw), F32)],
        compiler_params=_params(3),
        name="fox_flash",
    )(q, k, v, c, ct, og)


HEAD_ROWS = 16


def _split3(x):
    hi = x.astype(BF16)
    r1 = x - hi.astype(F32)
    mid = r1.astype(BF16)
    lo = (r1 - mid.astype(F32)).astype(BF16)
    return hi, mid, lo


def _fox_paged_kernel(pt_ref, q_ref, cn_ref, kn_ref, vn_ref, og_ref, kp_ref, vp_ref, lp_ref, o_ref,
                      qrows_ref, m_ref, l_ref, acc_ref, suf_ref, *, n_heads, steps, page):
    del pt_ref
    j = pl.program_id(1)
    sw = n_heads * HEAD_DIM
    n_rows = steps * HEAD_ROWS
    hrow = lax.broadcasted_iota(jnp.int32, (HEAD_ROWS, sw), 0)
    hcol = lax.broadcasted_iota(jnp.int32, (HEAD_ROWS, sw), 1) // HEAD_DIM
    head_mask = (hrow == hcol).astype(F32)

    @pl.when(j == 0)
    def _():
        q = q_ref[0].astype(F32)
        qrows = jnp.concatenate([q[t:t + 1] * head_mask for t in range(steps)], axis=0)
        qrows_ref[...] = qrows.astype(BF16)
        s = _dot_nt(qrows, kn_ref[0])
        cn = cn_ref[0][:, 0:8]
        trow = lax.broadcasted_iota(jnp.int32, (n_rows, 8), 0) // HEAD_ROWS
        tcol = lax.broadcasted_iota(jnp.int32, (n_rows, 8), 1)
        s = jnp.where(tcol <= trow, s + cn, NEG_BIG)
        m = jnp.max(s, axis=-1, keepdims=True)
        p = jnp.exp(s - m)
        m_ref[...] = m
        l_ref[...] = jnp.sum(p, axis=-1, keepdims=True)
        acc_ref[...] = _dot(p, vn_ref[0])
        suf_ref[...] = jnp.zeros_like(suf_ref)

    lp = lp_ref[0, 0]
    ri = lax.broadcasted_iota(jnp.int32, (page, page), 0)
    ci = lax.broadcasted_iota(jnp.int32, (page, page), 1)
    later = (ri > ci).astype(BF16)
    hi, mid, lo = _split3(lp)
    within = (jnp.dot(hi, later, preferred_element_type=F32) + jnp.dot(mid, later, preferred_element_type=F32)
              + jnp.dot(lo, later, preferred_element_type=F32))
    bias_h = within + suf_ref[...]
    suf_ref[...] = suf_ref[...] + jnp.sum(lp, axis=-1, keepdims=True)
    bias = jnp.concatenate([bias_h] * steps, axis=0) + cn_ref[0][:, 8:9]
    s = _dot_nt(qrows_ref[...], kp_ref[0, 0]) + bias
    m_old = m_ref[...]
    m_new = jnp.maximum(m_old, jnp.max(s, axis=-1, keepdims=True))
    alpha = jnp.exp(m_old - m_new)
    p = jnp.exp(s - m_new)
    l_ref[...] = alpha * l_ref[...] + jnp.sum(p, axis=-1, keepdims=True)
    acc_ref[...] = alpha * acc_ref[...] + _dot(p, vp_ref[0, 0])
    m_ref[...] = m_new

    @pl.when(j == pl.num_programs(1) - 1)
    def _():
        o = acc_ref[...] / l_ref[...]
        outs = [jnp.sum(o[t * HEAD_ROWS:(t + 1) * HEAD_ROWS] * head_mask, axis=0, keepdims=True) for t in range(steps)]
        o_ref[0] = jnp.concatenate(outs, axis=0) * og_ref[0]


def _fox_paged(page_table, q, cn, k_new, v_new, og, cache_k, cache_v, cache_lp_t, *, layer):
    b, steps, sw = q.shape
    n_heads = sw // HEAD_DIM
    n_pages = page_table.shape[1]
    page = cache_k.shape[2]
    n_rows = steps * HEAD_ROWS
    seq = lambda i, j, pt: (i, 0, 0)
    pg = lambda i, j, pt: (layer, pt[i, n_pages - 1 - j], 0, 0)
    grid_spec = pltpu.PrefetchScalarGridSpec(
        num_scalar_prefetch=1,
        grid=(b, n_pages),
        in_specs=[pl.BlockSpec((1, steps, sw), seq), pl.BlockSpec((1, n_rows, 16), seq),
                  pl.BlockSpec((1, 8, sw), seq), pl.BlockSpec((1, 8, sw), seq), pl.BlockSpec((1, steps, sw), seq),
                  pl.BlockSpec((1, 1, page, sw), pg), pl.BlockSpec((1, 1, page, sw), pg),
                  pl.BlockSpec((1, 1, HEAD_ROWS, page), pg)],
        out_specs=pl.BlockSpec((1, steps, sw), seq),
        scratch_shapes=[pltpu.VMEM((n_rows, sw), BF16), pltpu.VMEM((n_rows, 1), F32), pltpu.VMEM((n_rows, 1), F32),
                        pltpu.VMEM((n_rows, sw), F32), pltpu.VMEM((HEAD_ROWS, 1), F32)],
    )
    return pl.pallas_call(
        functools.partial(_fox_paged_kernel, n_heads=n_heads, steps=steps, page=page),
        grid_spec=grid_spec,
        out_shape=jax.ShapeDtypeStruct((b, steps, sw), F32),
        compiler_params=_params(2),
        name="fox_paged",
    )(page_table, q, cn, k_new, v_new, og, cache_k, cache_v, cache_lp_t)


def _to_time_major(x):
    x = jnp.swapaxes(x, 0, 1)
    return x.reshape((x.shape[0] * x.shape[1],) + x.shape[2:])


def _to_batch_major(x, b):
    return jnp.swapaxes(x.reshape((x.shape[0] // b, b) + x.shape[1:]), 0, 1)


def _trunk(h, P, *, prompt, n_seq, seq_len, mem_k, mem_v, rw_shift, rw_wkv, ffn_conv, fox_cache):
    depth = P["norm_gains"].shape[0]
    n, d = h.shape
    sw = P["rw_vecs"].shape[2]
    n_heads = sw // HEAD_DIM
    if prompt:
        tm, shift, blocks_per_seq = 256, 1, seq_len // 256
    else:
        tm, shift, blocks_per_seq = n, n_seq, 1
    v_first = None
    shifts, wkvs, fks, fvs, fls, convs = [], [], [], [], [], []
    for l in range(depth):
        gains = P["norm_gains"][l]
        i = l // 2
        if l % 2 == 0:
            init = jnp.zeros((n_seq, 1, d), F32) if prompt else rw_shift[i][None]
            vj = None if i == 0 else i - 1
            r, lw, k, v, kk, a, g, cq, sh = _rwkv_in(
                h, init, v_first, gains[0], P["rw_mix"][i], P["rw_w_in"][i], P["rw_w1"][i], P["rw_w2"][i],
                P["rw_a1"][i], P["rw_a2"][i], P["rw_g1"][i], P["rw_g2"][i],
                None if vj is None else P["rw_v1"][vj], None if vj is None else P["rw_v2"][vj],
                None if vj is None else P["rw_v0"][vj], P["rw_vecs"][i],
                tm=tm, shift=shift, blocks_per_seq=blocks_per_seq)
            if v_first is None:
                v_first = v
            if prompt:
                o_self, s_fin = _wkv(r, lw, k, v, kk, a, g, P["rw_vecs"][i], None, n_seq=n_seq, seq_len=seq_len,
                                     chunk=WKV_CHUNK, n_chunks=4)
                shifts.append(sh[:, 0])
            else:
                pad = lambda x: jnp.pad(_to_batch_major(x, n_seq), ((0, 0), (0, 8 - seq_len), (0, 0))).reshape(n_seq * 8, sw)
                o_pad, s_fin = _wkv(*[pad(x) for x in (r, lw, k, v, kk, a, g)], P["rw_vecs"][i], rw_wkv[i],
                                    n_seq=n_seq, seq_len=8, chunk=8, n_chunks=1)
                o_self = _to_time_major(o_pad.reshape(n_seq, 8, sw)[:, :seq_len])
                shifts.append(sh[0])
            wkvs.append(s_fin)
        else:
            q, k, v, lf, c, ct, og, cq = _fox_in(h, gains[0], P["fox_w_in"][i], P["fox_bf"][i], P["fox_qk_gain"][i],
                                                 tm=tm, shift=shift, steps=None if prompt else seq_len,
                                                 blocks_per_seq=blocks_per_seq)
            if prompt:
                o_self = _fox_flash(q, k, v, c, ct, og, n_seq=n_seq, seq_len=seq_len, tq=256, tk=512)
            else:
                cache_k, cache_v, cache_lp_t, page_table = fox_cache
                cb = _to_batch_major(c, n_seq)[:, :, :HEAD_ROWS]
                diff = cb[:, :, None, :] - cb[:, None, :, :]
                diff = jnp.pad(jnp.moveaxis(diff, 3, 2), ((0, 0), (0, 0), (0, 0), (0, 8 - seq_len)))
                cn = jnp.concatenate([diff, jnp.broadcast_to(cb[..., None], cb.shape + (8,))], axis=-1)
                cn = cn.reshape(n_seq, seq_len * HEAD_ROWS, 16)
                pad8 = lambda x: jnp.pad(_to_batch_major(x, n_seq), ((0, 0), (0, 8 - seq_len), (0, 0)))
                o_b = _fox_paged(page_table, _to_batch_major(q, n_seq), cn, pad8(k), pad8(v), _to_batch_major(og, n_seq),
                                 cache_k, cache_v, cache_lp_t, layer=i)
                o_self = _to_time_major(o_b)
            fks.append(k)
            fvs.append(v)
            fls.append(lf[:, :n_heads])
        if prompt:
            o_mem = _mem_attn(cq.reshape(n_seq, seq_len, MEM_WIDTH), mem_k[l], mem_v[l],
                              tq=min(1024, seq_len)).reshape(n, MEM_WIDTH)
        else:
            cq_b = jnp.pad(_to_batch_major(cq, n_seq), ((0, 0), (0, 16 - seq_len), (0, 0)))
            o_mem = _to_time_major(_mem_attn(cq_b, mem_k[l], mem_v[l], tq=16)[:, :seq_len])
        h = _out_proj(o_self, o_mem, h, P["w_out"][l], gains[1], tm=tm)
        if prompt:
            conv_init = jnp.zeros((n_seq, CONV_W - 1, P["ffn_conv_b"].shape[1]), F32)
        else:
            conv_init = _to_time_major(ffn_conv[l])[None]
        h, cs = _ffn(h, conv_init, gains[2], gains[3], P["ffn_w_up"][l], P["ffn_conv_w"][l], P["ffn_conv_b"][l],
                     P["ffn_w_down"][l], tm=tm, shift=shift, blocks_per_seq=blocks_per_seq)
        convs.append(cs if prompt else _to_batch_major(cs[0], n_seq))
    return h, wkvs, shifts, fks, fvs, fls, convs


def kernel(x_prompt, x_sample, cache_mem_k, cache_mem_v, cache_fox_k, cache_fox_v, cache_fox_logf, state_rwkv_wkv,
           state_rwkv_shift, state_ffn_conv, page_table, mem_prompt, norm_gains, w_out, mem_norm, w_mem_kv, rw_mix,
           rw_w_in, rw_vecs, rw_w1, rw_w2, rw_a1, rw_a2, rw_g1, rw_g2, rw_v0, rw_v1, rw_v2, fox_w_in, fox_bf,
           fox_qk_gain, ffn_w_up, ffn_conv_w, ffn_conv_b, ffn_w_down):
    P = dict(norm_gains=norm_gains, w_out=w_out, rw_mix=rw_mix, rw_w_in=rw_w_in, rw_vecs=rw_vecs, rw_w1=rw_w1,
             rw_w2=rw_w2, rw_a1=rw_a1, rw_a2=rw_a2, rw_g1=rw_g1, rw_g2=rw_g2, rw_v0=rw_v0, rw_v1=rw_v1, rw_v2=rw_v2,
             fox_w_in=fox_w_in, fox_bf=fox_bf, fox_qk_gain=fox_qk_gain, ffn_w_up=ffn_w_up, ffn_conv_w=ffn_conv_w,
             ffn_conv_b=ffn_conv_b, ffn_w_down=ffn_w_down)
    b, s, d = x_prompt.shape
    db, ds, _ = x_sample.shape
    depth = norm_gains.shape[0]
    n_heads = fox_bf.shape[1]
    sw = n_heads * HEAD_DIM
    d_ff = ffn_w_down.shape[1]

    p_mem_k, p_mem_v = _mem_kv(mem_prompt, mem_norm, w_mem_kv)
    hp, p_wkv, p_shift, p_fk, p_fv, p_fl, p_conv = _trunk(
        x_prompt.reshape(b * s, d), P, prompt=True, n_seq=b, seq_len=s, mem_k=p_mem_k, mem_v=p_mem_v,
        rw_shift=None, rw_wkv=None, ffn_conv=None, fox_cache=None)

    n_fox, n_pool, page, _, _ = cache_fox_k.shape
    cache_k = cache_fox_k.reshape(n_fox, n_pool, page, sw)
    cache_v = cache_fox_v.reshape(n_fox, n_pool, page, sw)
    cache_lp_t = jnp.pad(jnp.swapaxes(cache_fox_logf, 2, 3), ((0, 0), (0, 0), (0, HEAD_ROWS - n_heads), (0, 0)))
    hs, s_wkv, s_shift, s_fk, s_fv, s_fl, s_conv = _trunk(
        _to_time_major(x_sample), P, prompt=False, n_seq=db, seq_len=ds,
        mem_k=cache_mem_k.reshape(depth, db, N_MEM, MEM_WIDTH), mem_v=cache_mem_v.reshape(depth, db, N_MEM, MEM_WIDTH),
        rw_shift=state_rwkv_shift, rw_wkv=state_rwkv_wkv, ffn_conv=state_ffn_conv,
        fox_cache=(cache_k, cache_v, cache_lp_t, page_table))

    heads = lambda x, bb, tt: x.reshape(bb, tt, n_heads, HEAD_DIM)
    mem_heads = lambda x: x.reshape(depth, b, N_MEM, H_MEM, HEAD_DIM)
    return (hp.reshape(b, s, d), _to_batch_major(hs, db),
            jnp.stack(p_wkv), jnp.stack(p_shift),
            jnp.stack([heads(x, b, s) for x in p_fk]), jnp.stack([heads(x, b, s) for x in p_fv]),
            jnp.stack([x.reshape(b, s, n_heads) for x in p_fl]),
            mem_heads(p_mem_k), mem_heads(p_mem_v), jnp.stack(p_conv),
            jnp.stack(s_wkv), jnp.stack(s_shift),
            jnp.stack([heads(_to_batch_major(x, db), db, ds) for x in s_fk]),
            jnp.stack([heads(_to_batch_major(x, db), db, ds) for x in s_fv]),
            jnp.stack([_to_batch_major(x, db) for x in s_fl]),
            jnp.stack(s_conv))
```

```python
import functools

import jax
import jax.numpy as jnp
from jax import lax
from jax.experimental import pallas as pl
from jax.experimental.pallas import tpu as pltpu

F32 = jnp.float32
BF16 = jnp.bfloat16

HEAD_DIM = 64
N_MEM = 256
H_MEM = 4
MEM_WIDTH = H_MEM * HEAD_DIM
NORM_EPS = 1e-6
GN_EPS = 64e-5
ATTN_SCALE = HEAD_DIM ** -0.5
LOG2E = 1.4426950408889634
CONV_W = 3
LANES = 128
LORA_PAD = 128
VMEM_LIMIT = 56 * 1024 * 1024
NEG_BIG = -1e30
WKV_CHUNK = 64
WKV_HEADS = 4
FEAT = 8


def _params(n_axes):
    return pltpu.CompilerParams(dimension_semantics=("arbitrary",) * n_axes, vmem_limit_bytes=VMEM_LIMIT)


def _rms(x, g):
    return x * lax.rsqrt(jnp.mean(x * x, axis=-1, keepdims=True) + NORM_EPS) * g


def _dot(a, b):
    return jnp.dot(a.astype(BF16), b.astype(BF16), preferred_element_type=F32)


def _dot_nt(a, b):
    return lax.dot_general(a.astype(BF16), b.astype(BF16), (((1,), (1,)), ((), ())), preferred_element_type=F32)


def _dot_tn(a, b):
    return lax.dot_general(a.astype(BF16), b.astype(BF16), (((0,), (0,)), ((), ())), preferred_element_type=F32)


def _sigmoid(x):
    return 1.0 / (1.0 + jnp.exp(-x))


def _softplus(x):
    return jnp.maximum(x, 0.0) + jnp.log(1.0 + jnp.exp(-jnp.abs(x)))


def _split3(x):
    hi = x.astype(BF16)
    r1 = x - hi.astype(F32)
    mid = r1.astype(BF16)
    lo = (r1 - mid.astype(F32)).astype(BF16)
    return hi, mid, lo


def _shift_rows(x, carry, shift):
    n = x.shape[0]
    c = carry.shape[0]
    if shift % 8 == 0:
        return jnp.concatenate([carry[c - shift:], x[:n - shift]], axis=0)
    row = lax.broadcasted_iota(jnp.int32, x.shape, 0)
    out = pltpu.roll(x, shift, axis=0)
    for j in range(shift):
        out = jnp.where(row == j, carry[c - shift + j:c - shift + j + 1], out)
    return out


def _cumsum_chunks(x, c):
    pos = lax.broadcasted_iota(jnp.int32, x.shape, 0) & (c - 1)
    s = 1
    while s < c:
        x = x + jnp.where(pos >= s, pltpu.roll(x, s, axis=0), 0.0)
        s *= 2
    return x


def _const_spec(shape):
    nd = len(shape)
    return pl.BlockSpec(shape, lambda *_: (0,) * nd)


def _pad_cols(w, n):
    return jnp.pad(w, ((0, 0), (0, n - w.shape[1])))


def _pad_rows(w, n):
    return jnp.pad(w, ((0, n - w.shape[0]), (0, 0)))


def _memkv_kernel(mem_ref, gain_ref, w_ref, k_ref, v_ref):
    xn = _rms(mem_ref[0], gain_ref[0])
    kv = _dot(xn, w_ref[0])
    k_ref[0, 0] = kv[:, :MEM_WIDTH]
    v_ref[0, 0] = kv[:, MEM_WIDTH:]


def _mem_kv(mem, mem_norm, w_mem_kv):
    depth, d = mem_norm.shape
    b = mem.shape[0]
    out = jax.ShapeDtypeStruct((depth, b, N_MEM, MEM_WIDTH), F32)
    return pl.pallas_call(
        _memkv_kernel,
        grid=(depth, b),
        in_specs=[pl.BlockSpec((1, N_MEM, d), lambda l, i: (i, 0, 0)),
                  pl.BlockSpec((1, 1, d), lambda l, i: (l, 0, 0)),
                  pl.BlockSpec((1, d, 2 * MEM_WIDTH), lambda l, i: (l, 0, 0))],
        out_specs=[pl.BlockSpec((1, 1, N_MEM, MEM_WIDTH), lambda l, i: (l, i, 0, 0))] * 2,
        out_shape=[out, out],
        compiler_params=_params(2),
        name="mem_kv",
    )(mem, mem_norm.reshape(depth, 1, d), w_mem_kv.astype(BF16))


def _rwkv_in_kernel(*refs, sw, shift, blocks_per_seq, has_vfirst):
    if has_vfirst:
        (h_ref, init_ref, vfirst_ref, gain_ref, mix_ref, w_in_ref, w1_ref, w2_ref, a1_ref, a2_ref, g1_ref, g2_ref,
         v1_ref, v2_ref, v0_ref, vecs_ref,
         r_ref, lw_ref, k_ref, v_ref, kk_ref, a_ref, g_ref, cq_ref, shift_ref, carry_ref) = refs
    else:
        (h_ref, init_ref, gain_ref, mix_ref, w_in_ref, w1_ref, w2_ref, a1_ref, a2_ref, g1_ref, g2_ref, vecs_ref,
         r_ref, lw_ref, k_ref, v_ref, kk_ref, a_ref, g_ref, cq_ref, shift_ref, carry_ref) = refs

    @pl.when(pl.program_id(0) % blocks_per_seq == 0)
    def _():
        carry_ref[...] = init_ref[0]

    xn = _rms(h_ref[...], gain_ref[...])
    tm = xn.shape[0]
    prev = _shift_rows(xn, carry_ref[...], shift)
    carry_ref[...] = xn[tm - shift:]
    shift_ref[0] = xn[tm - shift:]
    xx = prev - xn
    xr, xw, xk, xv, xa, xg = [(xn + xx * mix_ref[j:j + 1]).astype(BF16) for j in range(6)]
    r = _dot(xr, w_in_ref[:, 0:sw])
    k = _dot(xk, w_in_ref[:, sw:2 * sw])
    v = _dot(xv, w_in_ref[:, 2 * sw:3 * sw])
    cq_ref[...] = _dot(xn, w_in_ref[:, 3 * sw:]).astype(BF16)
    w_pre = vecs_ref[0:1] + _dot(jnp.tanh(_dot(xw, w1_ref[...])), w2_ref[...])
    lw_ref[...] = -jnp.exp(-_softplus(-w_pre) - 0.5)
    a = _sigmoid(vecs_ref[1:2] + _dot(_dot(xa, a1_ref[...]), a2_ref[...]))
    if has_vfirst:
        nu = _sigmoid(v0_ref[...] + _dot(_dot(xv, v1_ref[...]), v2_ref[...]))
        v = v + (vfirst_ref[...] - v) * nu
    g_ref[...] = _dot(_sigmoid(_dot(xg, g1_ref[...])), g2_ref[...])
    r_ref[...] = r
    v_ref[...] = v
    a_ref[...] = a
    kk_ref[...] = k * vecs_ref[2:3]
    k_ref[...] = k * (1.0 + (a - 1.0) * vecs_ref[3:4])


def _rwkv_in(h, shift_init, v_first, gain, mix, w_in, w1, w2, a1, a2, g1, g2, v1, v2, v0, vecs, *, tm, shift,
             blocks_per_seq):
    n, d = h.shape
    sw = vecs.shape[1]
    n_seq = shift_init.shape[0]
    has_vfirst = v_first is not None
    row = lambda i: (i, 0)
    seq = lambda i: (i // blocks_per_seq, 0, 0)
    lora = lambda wa, wb: (_pad_cols(wa, LORA_PAD).astype(BF16), _pad_rows(wb, LORA_PAD).astype(BF16))
    w1p, w2p = lora(w1, w2)
    a1p, a2p = lora(a1, a2)
    g1p, g2p = lora(g1, g2)
    ins = [h, shift_init]
    specs = [pl.BlockSpec((tm, d), row), pl.BlockSpec((1, shift, d), seq)]
    if has_vfirst:
        ins.append(v_first)
        specs.append(pl.BlockSpec((tm, sw), row))
    consts = [gain.reshape(1, d), _pad_rows(mix, 8), w_in.astype(BF16), w1p, w2p, a1p, a2p, g1p, g2p]
    if has_vfirst:
        v1p, v2p = lora(v1, v2)
        consts += [v1p, v2p, v0.reshape(1, sw)]
    consts.append(_pad_rows(vecs, 8))
    ins += consts
    specs += [_const_spec(c.shape) for c in consts]
    wide = jax.ShapeDtypeStruct((n, sw), F32)
    outs = [wide] * 7 + [jax.ShapeDtypeStruct((n, MEM_WIDTH), BF16), jax.ShapeDtypeStruct((n_seq, shift, d), F32)]
    out_specs = [pl.BlockSpec((tm, sw), row)] * 7 + [pl.BlockSpec((tm, MEM_WIDTH), row),
                                                     pl.BlockSpec((1, shift, d), seq)]
    return pl.pallas_call(
        functools.partial(_rwkv_in_kernel, sw=sw, shift=shift, blocks_per_seq=blocks_per_seq, has_vfirst=has_vfirst),
        grid=(n // tm,),
        in_specs=specs, out_specs=out_specs, out_shape=outs,
        scratch_shapes=[pltpu.VMEM((shift, d), F32)],
        compiler_params=_params(1),
        name="rwkv_in",
    )(*ins)


def _wkv_kernel(*refs, chunk, n_chunks, heads, has_init):
    if has_init:
        (r_ref, lw_ref, k_ref, v_ref, kk_ref, a_ref, g_ref, vecs_ref, s0_ref, o_ref, sfin_ref, s_scr) = refs
    else:
        (r_ref, lw_ref, k_ref, v_ref, kk_ref, a_ref, g_ref, vecs_ref, o_ref, sfin_ref, s_scr) = refs
    tb = pl.program_id(2)
    c, nc, hd = chunk, n_chunks, HEAD_DIM
    n = c * nc

    @pl.when(tb == 0)
    def _():
        if has_init:
            s_scr[...] = s0_ref[0]
        else:
            s_scr[...] = jnp.zeros_like(s_scr)

    ri = lax.broadcasted_iota(jnp.int32, (n, n), 0)
    ci = lax.broadcasted_iota(jnp.int32, (n, n), 1)
    same = (ri // c) == (ci // c)
    incl = same & (ci <= ri)
    strict = same & (ci < ri)
    eye = (ci == ri).astype(F32)
    own = ((lax.broadcasted_iota(jnp.int32, (n, nc * hd), 0) // c)
           == (lax.broadcasted_iota(jnp.int32, (n, nc * hd), 1) // hd))
    n_double = max(c.bit_length() - 2, 0)

    def expand(x):
        return jnp.where(own, jnp.concatenate([x] * nc, axis=1), 0.0).astype(BF16)

    lw = lw_ref[...]
    cs = _cumsum_chunks(lw, c)
    g_inc_all = jnp.exp(cs)
    g_exc_all = jnp.exp(cs - lw)
    g_inv_all = jnp.exp(-cs)
    r_all, k_all, v_all = r_ref[...], k_ref[...], v_ref[...]
    kk_all, a_all, gate_all = kk_ref[...], a_ref[...], g_ref[...]

    hs = range(heads)
    sls = [slice(h * hd, (h + 1) * hd) for h in hs]
    r_ = [r_all[:, sl] for sl in sls]
    k_ = [k_all[:, sl] for sl in sls]
    v_ = [v_all[:, sl] for sl in sls]
    g_inc = [g_inc_all[:, sl] for sl in sls]
    kk_ = [kk_all[:, sl] for sl in sls]
    kk_ = [x / jnp.maximum(jnp.sqrt(jnp.sum(x * x, axis=-1, keepdims=True)), 1e-12) for x in kk_]
    rt = [r_[h] * g_inc[h] for h in hs]
    at = [-kk_[h] * g_exc_all[:, sls[h]] for h in hs]
    bt = [kk_[h] * a_all[:, sls[h]] * g_inv_all[:, sls[h]] for h in hs]
    kt = [k_[h] * g_inv_all[:, sls[h]] for h in hs]
    bk = [jnp.concatenate([bt[h], kt[h]], axis=0).astype(BF16) for h in hs]
    gram = [_dot_nt(jnp.concatenate([rt[h], at[h]], axis=0), bk[h]) for h in hs]
    m_rb = [jnp.where(incl, x[:n, :n], 0.0).astype(BF16) for x in gram]
    m_rk = [jnp.where(incl, x[:n, n:], 0.0).astype(BF16) for x in gram]
    m_ab = [jnp.where(strict, x[n:, :n], 0.0) for x in gram]
    m_ak = [jnp.where(strict, x[n:, n:], 0.0) for x in gram]
    inv = [eye + x for x in m_ab]
    p = m_ab
    for _ in range(n_double):
        p = [_dot(x, x) for x in p]
        inv = [inv[h] + _dot(inv[h], p[h]) for h in hs]
    inv = [x.astype(BF16) for x in inv]
    vb = [x.astype(BF16) for x in v_]
    w = [_dot(m_ak[h], vb[h]) for h in hs]
    a1 = [_dot(inv[h], at[h]) for h in hs]
    u = [_dot(inv[h], w[h]) for h in hs]
    r1 = [rt[h] + _dot(m_rb[h], a1[h]) for h in hs]
    y0 = [_dot(jnp.concatenate([m_rb[h], m_rk[h]], axis=1), jnp.concatenate([u[h].astype(BF16), vb[h]], axis=0))
          for h in hs]
    phi = [_dot_tn(expand(a1[h]), bt[h]) for h in hs]
    psi = [_dot_tn(jnp.concatenate([expand(u[h]), expand(v_[h])], axis=0), bk[h]) for h in hs]
    s = [s_scr[h] for h in hs]
    ys = [[] for _ in hs]
    for j in range(nc):
        rows = slice(j * c, (j + 1) * c)
        blk = slice(j * hd, (j + 1) * hd)
        for h in hs:
            ys[h].append(_dot_nt(r1[h][rows], s[h]) + y0[h][rows])
        s = [(s[h] + _dot(s[h], phi[h][blk]) + psi[h][blk]) * g_inc[h][(j + 1) * c - 1:(j + 1) * c] for h in hs]
    outs = []
    for h in hs:
        s_scr[h] = s[h]
        sl = sls[h]
        y = jnp.concatenate(ys[h], axis=0) if nc > 1 else ys[h][0]
        mu = jnp.mean(y, axis=-1, keepdims=True)
        var = jnp.mean(jnp.square(y - mu), axis=-1, keepdims=True)
        yn = (y - mu) * lax.rsqrt(var + GN_EPS) * vecs_ref[5:6, sl] + vecs_ref[6:7, sl]
        bonus = jnp.sum(r_[h] * k_[h] * vecs_ref[4:5, sl], axis=-1, keepdims=True) * v_[h]
        outs.append((yn + bonus) * gate_all[:, sl])
    o_ref[...] = jnp.concatenate(outs, axis=-1)

    @pl.when(tb == pl.num_programs(2) - 1)
    def _():
        sfin_ref[0] = s_scr[...]


def _wkv(r, lw, k, v, kk, a, g, vecs, s0, *, n_seq, seq_len, chunk, n_chunks):
    n, sw = r.shape
    n_heads = sw // HEAD_DIM
    width = WKV_HEADS * HEAD_DIM
    tb = chunk * n_chunks
    nb = seq_len // tb
    blk = pl.BlockSpec((tb, width), lambda b, hp, t: (b * nb + t, hp))
    ins = [r, lw, k, v, kk, a, g, _pad_rows(vecs, 8)]
    specs = [blk] * 7 + [pl.BlockSpec((8, width), lambda b, hp, t: (0, hp))]
    state_spec = pl.BlockSpec((1, WKV_HEADS, HEAD_DIM, HEAD_DIM), lambda b, hp, t: (b, hp, 0, 0))
    if s0 is not None:
        ins.append(s0)
        specs.append(state_spec)
    return pl.pallas_call(
        functools.partial(_wkv_kernel, chunk=chunk, n_chunks=n_chunks, heads=WKV_HEADS, has_init=s0 is not None),
        grid=(n_seq, n_heads // WKV_HEADS, nb),
        in_specs=specs,
        out_specs=[blk, state_spec],
        out_shape=[jax.ShapeDtypeStruct((n, sw), F32),
                   jax.ShapeDtypeStruct((n_seq, n_heads, HEAD_DIM, HEAD_DIM), F32)],
        scratch_shapes=[pltpu.VMEM((WKV_HEADS, HEAD_DIM, HEAD_DIM), F32)],
        compiler_params=_params(3),
        name="wkv",
    )(*ins)


def _mem_attn_kernel(q_ref, k_ref, v_ref, o_ref):
    q = q_ref[0]
    k = k_ref[0].astype(BF16)
    v = v_ref[0].astype(BF16)
    outs = []
    for h in range(H_MEM):
        sl = slice(h * HEAD_DIM, (h + 1) * HEAD_DIM)
        s = _dot_nt(q[:, sl], k[:, sl]) * ATTN_SCALE
        p = jnp.exp(s - jnp.max(s, axis=-1, keepdims=True))
        p = p / jnp.sum(p, axis=-1, keepdims=True)
        outs.append(_dot(p, v[:, sl]))
    o_ref[0] = jnp.concatenate(outs, axis=-1).astype(BF16)


def _mem_attn(cq, mem_k, mem_v, *, tq):
    n_seq, t, _ = cq.shape
    qspec = pl.BlockSpec((1, tq, MEM_WIDTH), lambda b, i: (b, i, 0))
    mspec = pl.BlockSpec((1, N_MEM, MEM_WIDTH), lambda b, i: (b, 0, 0))
    return pl.pallas_call(
        _mem_attn_kernel,
        grid=(n_seq, t // tq),
        in_specs=[qspec, mspec, mspec],
        out_specs=qspec,
        out_shape=jax.ShapeDtypeStruct(cq.shape, BF16),
        compiler_params=_params(2),
        name="mem_attn",
    )(cq, mem_k, mem_v)


def _out_proj_kernel(os_ref, om_ref, h_ref, w_ref, gain_ref, o_ref, *, sw):
    mixed = _dot(os_ref[...], w_ref[0:sw, :]) + _dot(om_ref[...], w_ref[sw:, :])
    o_ref[...] = h_ref[...] + _rms(mixed, gain_ref[...])


def _out_proj(o_self, o_mem, h, w_out, gain, *, tm):
    n, d = h.shape
    sw = o_self.shape[1]
    row = lambda i: (i, 0)
    return pl.pallas_call(
        functools.partial(_out_proj_kernel, sw=sw),
        grid=(n // tm,),
        in_specs=[pl.BlockSpec((tm, sw), row), pl.BlockSpec((tm, MEM_WIDTH), row), pl.BlockSpec((tm, d), row),
                  _const_spec(w_out.shape), _const_spec((1, d))],
        out_specs=pl.BlockSpec((tm, d), row),
        out_shape=jax.ShapeDtypeStruct((n, d), F32),
        compiler_params=_params(1),
        name="out_proj",
    )(o_self, o_mem, h, w_out.astype(BF16), gain.reshape(1, d))


def _gelu_tanh(x):
    return 0.5 * x * (1.0 + jnp.tanh(0.7978845608028654 * (x + 0.044715 * (x * x * x))))


def _ffn_kernel(h_ref, init_ref, gin_ref, gout_ref, wup_ref, cw_ref, cb_ref, wdn_ref, o_ref, conv_ref, carry_ref,
                acc_ref, *, d_ff, shift, blocks_per_seq, ff_chunk):
    @pl.when(pl.program_id(0) % blocks_per_seq == 0)
    def _():
        carry_ref[...] = init_ref[0]

    h = h_ref[...]
    tm = h.shape[0]
    xn = _rms(h, gin_ref[...]).astype(BF16)
    for j in range(d_ff // ff_chunk):
        cols = slice(j * ff_chunk, (j + 1) * ff_chunk)
        u = _dot(xn, wup_ref[:, cols])
        g = _dot(xn, wup_ref[:, d_ff + j * ff_chunk:d_ff + (j + 1) * ff_chunk])
        carry = carry_ref[:, cols]
        g1 = _shift_rows(g, carry, shift)
        g2 = _shift_rows(g1, carry[:shift], shift)
        gc = cb_ref[:, cols] + g2 * cw_ref[0:1, cols] + g1 * cw_ref[1:2, cols] + g * cw_ref[2:3, cols]
        carry_ref[:, cols] = g[tm - 2 * shift:]
        conv_ref[0, :, cols] = g[tm - 2 * shift:]
        part = _dot(_gelu_tanh(gc) * u, wdn_ref[cols, :])
        if j == 0:
            acc_ref[...] = part
        else:
            acc_ref[...] += part
    o_ref[...] = h + _rms(acc_ref[...], gout_ref[...])


def _ffn(h, conv_init, gain_in, gain_out, w_up, conv_w, conv_b, w_down, *, tm, shift, blocks_per_seq):
    n, d = h.shape
    d_ff = w_down.shape[0]
    n_seq = conv_init.shape[0]
    row = lambda i: (i, 0)
    seq = lambda i: (i // blocks_per_seq, 0, 0)
    once = pl.Buffered(1)
    return pl.pallas_call(
        functools.partial(_ffn_kernel, d_ff=d_ff, shift=shift, blocks_per_seq=blocks_per_seq, ff_chunk=2 * LANES),
        grid=(n // tm,),
        in_specs=[pl.BlockSpec((tm, d), row), pl.BlockSpec((1, 2 * shift, d_ff), seq),
                  _const_spec((1, d)), _const_spec((1, d)),
                  pl.BlockSpec(w_up.shape, lambda i: (0, 0), pipeline_mode=once),
                  _const_spec((8, d_ff)), _const_spec((1, d_ff)),
                  pl.BlockSpec(w_down.shape, lambda i: (0, 0), pipeline_mode=once)],
        out_specs=[pl.BlockSpec((tm, d), row), pl.BlockSpec((1, 2 * shift, d_ff), seq)],
        out_shape=[jax.ShapeDtypeStruct((n, d), F32), jax.ShapeDtypeStruct((n_seq, 2 * shift, d_ff), F32)],
        scratch_shapes=[pltpu.VMEM((2 * shift, d_ff), F32), pltpu.VMEM((tm, d), F32)],
        compiler_params=_params(1),
        name="ffn",
    )(h, conv_init, gain_in.reshape(1, d), gain_out.reshape(1, d), w_up.astype(BF16), _pad_rows(conv_w, 8),
      conv_b.reshape(1, d_ff), w_down.astype(BF16))


def _head_mean_sq(x, ones_bd):
    outs = []
    for j in range(x.shape[1] // 256):
        sq = jnp.square(x[:, j * 256:(j + 1) * 256])
        hi = sq.astype(BF16)
        lo = (sq - hi.astype(F32)).astype(BF16)
        outs.append(jnp.dot(hi, ones_bd, preferred_element_type=F32) + jnp.dot(lo, ones_bd, preferred_element_type=F32))
    return jnp.concatenate(outs, axis=-1) * (1.0 / HEAD_DIM)


def _feat_base(h):
    return FEAT * (h // 2) + HEAD_DIM * (1 - h % 2)


def _bias_features(c, n_heads):
    terms = jnp.concatenate(_split3(c * LOG2E), axis=1)
    r = lax.broadcasted_iota(jnp.int32, (3 * LANES, LANES), 0)
    l = lax.broadcasted_iota(jnp.int32, (3 * LANES, LANES), 1)
    head = r % LANES
    off = l - _feat_base(head)
    in_head = (off >= 0) & (off < FEAT) & (head < n_heads)
    place_q = (in_head & (off == r // LANES)).astype(BF16)
    place_k = (in_head & (off == 3 + r // LANES)).astype(BF16)
    lane = lax.broadcasted_iota(jnp.int32, (1, LANES), 1)
    slot = lane % FEAT
    used = ((lane % HEAD_DIM) // FEAT) < (n_heads // 2)
    ones_q = (used & (slot >= 3) & (slot < 6)).astype(F32)
    ones_k = (used & (slot < 3)).astype(F32)
    qf = jnp.dot(terms, place_q, preferred_element_type=F32) + ones_q
    kf = ones_k - jnp.dot(terms, place_k, preferred_element_type=F32)
    return qf.astype(BF16), kf.astype(BF16)


def _fox_in_kernel(h_ref, gain_ref, w_ref, bf_ref, qkg_ref, q_ref, k_ref, v_ref, kb_ref, vb_ref, lf_ref, c_ref,
                   qf_ref, kf_ref, og_ref, cq_ref, carry_ref, *, sw, n_heads, shift, steps, blocks_per_seq):
    xn = _rms(h_ref[...], gain_ref[...]).astype(BF16)
    tm = xn.shape[0]
    ri = lax.broadcasted_iota(jnp.int32, (256, 256), 0) // HEAD_DIM
    ci = lax.broadcasted_iota(jnp.int32, (256, 256), 1) // HEAD_DIM
    ones_bd = (ri == ci).astype(BF16)
    q = jnp.dot(xn, w_ref[:, 0:sw], preferred_element_type=F32)
    q = q * lax.rsqrt(_head_mean_sq(q, ones_bd) + NORM_EPS) * qkg_ref[0:1]
    q_ref[...] = (q * (ATTN_SCALE * LOG2E)).astype(BF16)
    k = jnp.dot(xn, w_ref[:, sw:2 * sw], preferred_element_type=F32)
    k = k * lax.rsqrt(_head_mean_sq(k, ones_bd) + NORM_EPS) * qkg_ref[1:2]
    k_ref[...] = k
    kb_ref[...] = k.astype(BF16)
    v = jnp.dot(xn, w_ref[:, 2 * sw:3 * sw], preferred_element_type=F32)
    v_ref[...] = v
    vb_ref[...] = v.astype(BF16)
    o0 = 3 * sw + LANES
    lf = -_softplus(-(jnp.dot(xn, w_ref[:, 3 * sw:o0], preferred_element_type=F32) + bf_ref[...]))
    lf_ref[...] = lf
    og_ref[...] = _sigmoid(jnp.dot(xn, w_ref[:, o0:o0 + sw], preferred_element_type=F32))
    cq_ref[...] = jnp.dot(xn, w_ref[:, o0 + sw:], preferred_element_type=F32).astype(BF16)
    if steps is None:
        @pl.when(pl.program_id(0) % blocks_per_seq == 0)
        def _():
            carry_ref[...] = jnp.zeros_like(carry_ref)

        c = _cumsum_chunks(lf, tm) + carry_ref[0:1]
        carry_ref[...] = jnp.broadcast_to(c[tm - 1:tm], carry_ref.shape)
    else:
        parts = [lf[0:shift]]
        for t in range(1, steps):
            parts.append(parts[-1] + lf[t * shift:(t + 1) * shift])
        c = jnp.concatenate(parts, axis=0)
    c_ref[...] = c
    qf, kf = _bias_features(c, n_heads)
    qf_ref[...] = qf
    kf_ref[...] = kf


def _fox_in(h, gain, w_in, bf, qk_gain, *, tm, shift, steps, blocks_per_seq):
    n, d = h.shape
    n_heads = bf.shape[0]
    sw = n_heads * HEAD_DIM
    w = jnp.concatenate([w_in[:, :3 * sw], _pad_cols(w_in[:, 3 * sw:3 * sw + n_heads], LANES),
                         w_in[:, 3 * sw + n_heads:]], axis=1).astype(BF16)
    qkg = _pad_rows(jnp.tile(qk_gain, (1, n_heads)), 8)
    row = lambda i: (i, 0)
    wide = pl.BlockSpec((tm, sw), row)
    lane = pl.BlockSpec((tm, LANES), row)
    wide_f32 = jax.ShapeDtypeStruct((n, sw), F32)
    wide_bf16 = jax.ShapeDtypeStruct((n, sw), BF16)
    lane_f32 = jax.ShapeDtypeStruct((n, LANES), F32)
    lane_bf16 = jax.ShapeDtypeStruct((n, LANES), BF16)
    return pl.pallas_call(
        functools.partial(_fox_in_kernel, sw=sw, n_heads=n_heads, shift=shift, steps=steps,
                          blocks_per_seq=blocks_per_seq),
        grid=(n // tm,),
        in_specs=[pl.BlockSpec((tm, d), row), _const_spec((1, d)), _const_spec(w.shape), _const_spec((1, LANES)),
                  _const_spec((8, sw))],
        out_specs=[wide, wide, wide, wide, wide, lane, lane, lane, lane, wide, pl.BlockSpec((tm, MEM_WIDTH), row)],
        out_shape=[wide_bf16, wide_f32, wide_f32, wide_bf16, wide_bf16, lane_f32, lane_f32, lane_bf16, lane_bf16,
                   wide_f32, jax.ShapeDtypeStruct((n, MEM_WIDTH), BF16)],
        scratch_shapes=[pltpu.VMEM((8, LANES), F32)],
        compiler_params=_params(1),
        name="fox_in",
    )(h, gain.reshape(1, d), w, _pad_cols(bf.reshape(1, n_heads), LANES), qkg)


def _fox_flash_kernel(q_ref, k_ref, v_ref, qf_ref, kf_ref, og_ref, o_ref, m_ref, l_ref, acc_ref, *, n_heads, tq, tk):
    qi = pl.program_id(1)
    ki = pl.program_id(2)
    last = ((qi + 1) * tq - 1) // tk

    @pl.when(ki == 0)
    def _():
        m_ref[...] = jnp.full_like(m_ref, NEG_BIG)
        l_ref[...] = jnp.zeros_like(l_ref)
        acc_ref[...] = jnp.zeros_like(acc_ref)

    lane = lax.broadcasted_iota(jnp.int32, (1, LANES), 1)
    upper = lane >= HEAD_DIM
    zero = jnp.zeros((), BF16)

    def step(masked):
        if masked:
            qpos = qi * tq + lax.broadcasted_iota(jnp.int32, (tq, tk), 0)
            kpos = ki * tk + lax.broadcasted_iota(jnp.int32, (tq, tk), 1)
            causal = kpos <= qpos
        qf = qf_ref[...]
        kf = kf_ref[...]
        for pr in range(n_heads // 2):
            cols = slice(pr * LANES, (pr + 1) * LANES)
            heads = (2 * pr, 2 * pr + 1)
            q2, k2, v2 = q_ref[:, cols], k_ref[:, cols], v_ref[:, cols]
            own = [upper == bool(h % 2) for h in heads]
            feat = [(lane >= _feat_base(h)) & (lane < _feat_base(h) + FEAT) for h in heads]
            q_aug = [jnp.where(own[i], q2, jnp.where(feat[i], qf, zero)) for i in range(2)]
            k_aug = [jnp.where(own[i], k2, kf) for i in range(2)]
            s = [_dot_nt(q_aug[i], k_aug[i]) for i in range(2)]
            if masked:
                s = [jnp.where(causal, x, NEG_BIG) for x in s]
            m_old = [m_ref[:, h:h + 1] for h in heads]
            m_new = [jnp.maximum(m_old[i], jnp.max(s[i], axis=-1, keepdims=True)) for i in range(2)]
            alpha = [jnp.exp2(m_old[i] - m_new[i]) for i in range(2)]
            p = [jnp.exp2(s[i] - m_new[i]) for i in range(2)]
            for i, h in enumerate(heads):
                l_ref[:, h:h + 1] = alpha[i] * l_ref[:, h:h + 1] + jnp.sum(p[i], axis=-1, keepdims=True)
                m_ref[:, h:h + 1] = m_new[i]
            pv = [jnp.dot(p[i].astype(BF16), v2, preferred_element_type=F32) for i in range(2)]
            acc = acc_ref[:, cols]
            acc_ref[:, cols] = jnp.where(upper, alpha[1] * acc + pv[1], alpha[0] * acc + pv[0])

    @pl.when(ki < last)
    def _():
        step(False)

    @pl.when(ki == last)
    def _():
        step(True)
        for pr in range(n_heads // 2):
            cols = slice(pr * LANES, (pr + 1) * LANES)
            inv_l = jnp.where(upper, 1.0 / l_ref[:, 2 * pr + 1:2 * pr + 2], 1.0 / l_ref[:, 2 * pr:2 * pr + 1])
            o_ref[:, cols] = acc_ref[:, cols] * inv_l * og_ref[:, cols]


def _fox_flash(q, k, v, qf, kf, og, *, n_seq, seq_len, tq, tk):
    n, sw = k.shape
    n_heads = sw // HEAD_DIM
    nq, nk = seq_len // tq, seq_len // tk
    qrow = lambda b, i, j: (b * nq + i, 0)
    krow = lambda b, i, j: (b * nk + jnp.minimum(j, ((i + 1) * tq - 1) // tk), 0)
    return pl.pallas_call(
        functools.partial(_fox_flash_kernel, n_heads=n_heads, tq=tq, tk=tk),
        grid=(n_seq, nq, nk),
        in_specs=[pl.BlockSpec((tq, sw), qrow), pl.BlockSpec((tk, sw), krow), pl.BlockSpec((tk, sw), krow),
                  pl.BlockSpec((tq, LANES), qrow), pl.BlockSpec((tk, LANES), krow), pl.BlockSpec((tq, sw), qrow)],
        out_specs=pl.BlockSpec((tq, sw), qrow),
        out_shape=jax.ShapeDtypeStruct((n, sw), F32),
        scratch_shapes=[pltpu.VMEM((tq, LANES), F32), pltpu.VMEM((tq, LANES), F32), pltpu.VMEM((tq, sw), F32)],
        compiler_params=_params(3),
        name="fox_flash",
    )(q, k, v, qf, kf, og)


HEAD_ROWS = 16


def _fox_paged_kernel(pt_ref, q_ref, cn_ref, kn_ref, vn_ref, og_ref, kp_ref, vp_ref, lp_ref, o_ref,
                      qrows_ref, m_ref, l_ref, acc_ref, suf_ref, *, n_heads, steps, page):
    del pt_ref
    j = pl.program_id(1)
    sw = n_heads * HEAD_DIM
    n_rows = steps * HEAD_ROWS
    hrow = lax.broadcasted_iota(jnp.int32, (HEAD_ROWS, sw), 0)
    hcol = lax.broadcasted_iota(jnp.int32, (HEAD_ROWS, sw), 1) // HEAD_DIM
    head_mask = (hrow == hcol).astype(F32)

    @pl.when(j == 0)
    def _():
        q = q_ref[0].astype(F32)
        qrows = jnp.concatenate([q[t:t + 1] * head_mask for t in range(steps)], axis=0)
        qrows_ref[...] = qrows.astype(BF16)
        s = _dot_nt(qrows, kn_ref[0])
        cn = cn_ref[0][:, 0:8]
        trow = lax.broadcasted_iota(jnp.int32, (n_rows, 8), 0) // HEAD_ROWS
        tcol = lax.broadcasted_iota(jnp.int32, (n_rows, 8), 1)
        s = jnp.where(tcol <= trow, s + LOG2E * cn, NEG_BIG)
        m = jnp.max(s, axis=-1, keepdims=True)
        p = jnp.exp2(s - m)
        m_ref[...] = m
        l_ref[...] = jnp.sum(p, axis=-1, keepdims=True)
        acc_ref[...] = _dot(p, vn_ref[0])
        suf_ref[...] = jnp.zeros_like(suf_ref)

    lp = lp_ref[0, 0]
    ri = lax.broadcasted_iota(jnp.int32, (page, page), 0)
    ci = lax.broadcasted_iota(jnp.int32, (page, page), 1)
    later = (ri > ci).astype(BF16)
    hi, mid, lo = _split3(lp)
    within = (jnp.dot(hi, later, preferred_element_type=F32) + jnp.dot(mid, later, preferred_element_type=F32)
              + jnp.dot(lo, later, preferred_element_type=F32))
    bias_h = within + suf_ref[...]
    suf_ref[...] = suf_ref[...] + jnp.sum(lp, axis=-1, keepdims=True)
    bias = jnp.concatenate([bias_h] * steps, axis=0) + cn_ref[0][:, 8:9]
    s = _dot_nt(qrows_ref[...], kp_ref[0, 0]) + LOG2E * bias
    m_old = m_ref[...]
    m_new = jnp.maximum(m_old, jnp.max(s, axis=-1, keepdims=True))
    alpha = jnp.exp2(m_old - m_new)
    p = jnp.exp2(s - m_new)
    l_ref[...] = alpha * l_ref[...] + jnp.sum(p, axis=-1, keepdims=True)
    acc_ref[...] = alpha * acc_ref[...] + _dot(p, vp_ref[0, 0])
    m_ref[...] = m_new

    @pl.when(j == pl.num_programs(1) - 1)
    def _():
        o = acc_ref[...] / l_ref[...]
        outs = [jnp.sum(o[t * HEAD_ROWS:(t + 1) * HEAD_ROWS] * head_mask, axis=0, keepdims=True) for t in range(steps)]
        o_ref[0] = jnp.concatenate(outs, axis=0) * og_ref[0]


def _fox_paged(page_table, q, cn, k_new, v_new, og, cache_k, cache_v, cache_lp_t, *, layer):
    b, steps, sw = q.shape
    n_heads = sw // HEAD_DIM
    n_pages = page_table.shape[1]
    page = cache_k.shape[2]
    n_rows = steps * HEAD_ROWS
    seq = lambda i, j, pt: (i, 0, 0)
    pg = lambda i, j, pt: (layer, pt[i, n_pages - 1 - j], 0, 0)
    grid_spec = pltpu.PrefetchScalarGridSpec(
        num_scalar_prefetch=1,
        grid=(b, n_pages),
        in_specs=[pl.BlockSpec((1, steps, sw), seq), pl.BlockSpec((1, n_rows, 16), seq),
                  pl.BlockSpec((1, 8, sw), seq), pl.BlockSpec((1, 8, sw), seq), pl.BlockSpec((1, steps, sw), seq),
                  pl.BlockSpec((1, 1, page, sw), pg), pl.BlockSpec((1, 1, page, sw), pg),
                  pl.BlockSpec((1, 1, HEAD_ROWS, page), pg)],
        out_specs=pl.BlockSpec((1, steps, sw), seq),
        scratch_shapes=[pltpu.VMEM((n_rows, sw), BF16), pltpu.VMEM((n_rows, 1), F32), pltpu.VMEM((n_rows, 1), F32),
                        pltpu.VMEM((n_rows, sw), F32), pltpu.VMEM((HEAD_ROWS, 1), F32)],
    )
    return pl.pallas_call(
        functools.partial(_fox_paged_kernel, n_heads=n_heads, steps=steps, page=page),
        grid_spec=grid_spec,
        out_shape=jax.ShapeDtypeStruct((b, steps, sw), F32),
        compiler_params=_params(2),
        name="fox_paged",
    )(page_table, q, cn, k_new, v_new, og, cache_k, cache_v, cache_lp_t)


def _to_time_major(x):
    x = jnp.swapaxes(x, 0, 1)
    return x.reshape((x.shape[0] * x.shape[1],) + x.shape[2:])


def _to_batch_major(x, b):
    return jnp.swapaxes(x.reshape((x.shape[0] // b, b) + x.shape[1:]), 0, 1)


def _trunk(h, P, *, prompt, n_seq, seq_len, mem_k, mem_v, rw_shift, rw_wkv, ffn_conv, fox_cache):
    depth = P["norm_gains"].shape[0]
    n, d = h.shape
    sw = P["rw_vecs"].shape[2]
    n_heads = sw // HEAD_DIM
    if prompt:
        tm, shift, blocks_per_seq = 256, 1, seq_len // 256
    else:
        tm, shift, blocks_per_seq = n, n_seq, 1
    v_first = None
    shifts, wkvs, fks, fvs, fls, convs = [], [], [], [], [], []
    for l in range(depth):
        gains = P["norm_gains"][l]
        i = l // 2
        if l % 2 == 0:
            init = jnp.zeros((n_seq, 1, d), F32) if prompt else rw_shift[i][None]
            vj = None if i == 0 else i - 1
            r, lw, k, v, kk, a, g, cq, sh = _rwkv_in(
                h, init, v_first, gains[0], P["rw_mix"][i], P["rw_w_in"][i], P["rw_w1"][i], P["rw_w2"][i],
                P["rw_a1"][i], P["rw_a2"][i], P["rw_g1"][i], P["rw_g2"][i],
                None if vj is None else P["rw_v1"][vj], None if vj is None else P["rw_v2"][vj],
                None if vj is None else P["rw_v0"][vj], P["rw_vecs"][i],
                tm=tm, shift=shift, blocks_per_seq=blocks_per_seq)
            if v_first is None:
                v_first = v
            if prompt:
                o_self, s_fin = _wkv(r, lw, k, v, kk, a, g, P["rw_vecs"][i], None, n_seq=n_seq, seq_len=seq_len,
                                     chunk=WKV_CHUNK, n_chunks=4)
                shifts.append(sh[:, 0])
            else:
                pad = lambda x: jnp.pad(_to_batch_major(x, n_seq), ((0, 0), (0, 8 - seq_len), (0, 0))).reshape(n_seq * 8, sw)
                o_pad, s_fin = _wkv(*[pad(x) for x in (r, lw, k, v, kk, a, g)], P["rw_vecs"][i], rw_wkv[i],
                                    n_seq=n_seq, seq_len=8, chunk=8, n_chunks=1)
                o_self = _to_time_major(o_pad.reshape(n_seq, 8, sw)[:, :seq_len])
                shifts.append(sh[0])
            wkvs.append(s_fin)
        else:
            q, k, v, kb, vb, lf, c, qf, kf, og, cq = _fox_in(
                h, gains[0], P["fox_w_in"][i], P["fox_bf"][i], P["fox_qk_gain"][i],
                tm=tm, shift=shift, steps=None if prompt else seq_len, blocks_per_seq=blocks_per_seq)
            if prompt:
                o_self = _fox_flash(q, kb, vb, qf, kf, og, n_seq=n_seq, seq_len=seq_len, tq=128,
                                    tk=min(1024, seq_len))
            else:
                cache_k, cache_v, cache_lp_t, page_table = fox_cache
                cb = _to_batch_major(c, n_seq)[:, :, :HEAD_ROWS]
                diff = cb[:, :, None, :] - cb[:, None, :, :]
                diff = jnp.pad(jnp.moveaxis(diff, 3, 2), ((0, 0), (0, 0), (0, 0), (0, 8 - seq_len)))
                cn = jnp.concatenate([diff, jnp.broadcast_to(cb[..., None], cb.shape + (8,))], axis=-1)
                cn = cn.reshape(n_seq, seq_len * HEAD_ROWS, 16)
                pad8 = lambda x: jnp.pad(_to_batch_major(x, n_seq), ((0, 0), (0, 8 - seq_len), (0, 0)))
                o_b = _fox_paged(page_table, _to_batch_major(q, n_seq), cn, pad8(k), pad8(v), _to_batch_major(og, n_seq),
                                 cache_k, cache_v, cache_lp_t, layer=i)
                o_self = _to_time_major(o_b)
            fks.append(k)
            fvs.append(v)
            fls.append(lf[:, :n_heads])
        if prompt:
            o_mem = _mem_attn(cq.reshape(n_seq, seq_len, MEM_WIDTH), mem_k[l], mem_v[l],
                              tq=min(1024, seq_len)).reshape(n, MEM_WIDTH)
        else:
            cq_b = jnp.pad(_to_batch_major(cq, n_seq), ((0, 0), (0, 16 - seq_len), (0, 0)))
            o_mem = _to_time_major(_mem_attn(cq_b, mem_k[l], mem_v[l], tq=16)[:, :seq_len])
        h = _out_proj(o_self, o_mem, h, P["w_out"][l], gains[1], tm=tm)
        if prompt:
            conv_init = jnp.zeros((n_seq, CONV_W - 1, P["ffn_conv_b"].shape[1]), F32)
        else:
            conv_init = _to_time_major(ffn_conv[l])[None]
        h, cs = _ffn(h, conv_init, gains[2], gains[3], P["ffn_w_up"][l], P["ffn_conv_w"][l], P["ffn_conv_b"][l],
                     P["ffn_w_down"][l], tm=tm, shift=shift, blocks_per_seq=blocks_per_seq)
        convs.append(cs if prompt else _to_batch_major(cs[0], n_seq))
    return h, wkvs, shifts, fks, fvs, fls, convs


def kernel(x_prompt, x_sample, cache_mem_k, cache_mem_v, cache_fox_k, cache_fox_v, cache_fox_logf, state_rwkv_wkv,
           state_rwkv_shift, state_ffn_conv, page_table, mem_prompt, norm_gains, w_out, mem_norm, w_mem_kv, rw_mix,
           rw_w_in, rw_vecs, rw_w1, rw_w2, rw_a1, rw_a2, rw_g1, rw_g2, rw_v0, rw_v1, rw_v2, fox_w_in, fox_bf,
           fox_qk_gain, ffn_w_up, ffn_conv_w, ffn_conv_b, ffn_w_down):
    P = dict(norm_gains=norm_gains, w_out=w_out, rw_mix=rw_mix, rw_w_in=rw_w_in, rw_vecs=rw_vecs, rw_w1=rw_w1,
             rw_w2=rw_w2, rw_a1=rw_a1, rw_a2=rw_a2, rw_g1=rw_g1, rw_g2=rw_g2, rw_v0=rw_v0, rw_v1=rw_v1, rw_v2=rw_v2,
             fox_w_in=fox_w_in, fox_bf=fox_bf, fox_qk_gain=fox_qk_gain, ffn_w_up=ffn_w_up, ffn_conv_w=ffn_conv_w,
             ffn_conv_b=ffn_conv_b, ffn_w_down=ffn_w_down)
    b, s, d = x_prompt.shape
    db, ds, _ = x_sample.shape
    depth = norm_gains.shape[0]
    n_heads = fox_bf.shape[1]
    sw = n_heads * HEAD_DIM

    p_mem_k, p_mem_v = _mem_kv(mem_prompt, mem_norm, w_mem_kv)
    hp, p_wkv, p_shift, p_fk, p_fv, p_fl, p_conv = _trunk(
        x_prompt.reshape(b * s, d), P, prompt=True, n_seq=b, seq_len=s, mem_k=p_mem_k, mem_v=p_mem_v,
        rw_shift=None, rw_wkv=None, ffn_conv=None, fox_cache=None)

    n_fox, n_pool, page, _, _ = cache_fox_k.shape
    cache_k = cache_fox_k.reshape(n_fox, n_pool, page, sw)
    cache_v = cache_fox_v.reshape(n_fox, n_pool, page, sw)
    cache_lp_t = jnp.pad(jnp.swapaxes(cache_fox_logf, 2, 3), ((0, 0), (0, 0), (0, HEAD_ROWS - n_heads), (0, 0)))
    hs, s_wkv, s_shift, s_fk, s_fv, s_fl, s_conv = _trunk(
        _to_time_major(x_sample), P, prompt=False, n_seq=db, seq_len=ds,
        mem_k=cache_mem_k.reshape(depth, db, N_MEM, MEM_WIDTH), mem_v=cache_mem_v.reshape(depth, db, N_MEM, MEM_WIDTH),
        rw_shift=state_rwkv_shift, rw_wkv=state_rwkv_wkv, ffn_conv=state_ffn_conv,
        fox_cache=(cache_k, cache_v, cache_lp_t, page_table))

    heads = lambda x, bb, tt: x.reshape(bb, tt, n_heads, HEAD_DIM)
    mem_heads = lambda x: x.reshape(depth, b, N_MEM, H_MEM, HEAD_DIM)
    return (hp.reshape(b, s, d), _to_batch_major(hs, db),
            jnp.stack(p_wkv), jnp.stack(p_shift),
            jnp.stack([heads(x, b, s) for x in p_fk]), jnp.stack([heads(x, b, s) for x in p_fv]),
            jnp.stack([x.reshape(b, s, n_heads) for x in p_fl]),
            mem_heads(p_mem_k), mem_heads(p_mem_v), jnp.stack(p_conv),
            jnp.stack(s_wkv), jnp.stack(s_shift),
            jnp.stack([heads(_to_batch_major(x, db), db, ds) for x in s_fk]),
            jnp.stack([heads(_to_batch_major(x, db), db, ds) for x in s_fv]),
            jnp.stack([_to_batch_major(x, db) for x in s_fl]),
            jnp.stack(s_conv))
```

```python
import functools

import jax
import jax.numpy as jnp
from jax import lax
from jax.experimental import pallas as pl
from jax.experimental.pallas import tpu as pltpu

F32 = jnp.float32
BF16 = jnp.bfloat16

HEAD_DIM = 64
N_MEM = 256
H_MEM = 4
MEM_WIDTH = H_MEM * HEAD_DIM
NORM_EPS = 1e-6
GN_EPS = 64e-5
ATTN_SCALE = HEAD_DIM ** -0.5
LOG2E = 1.4426950408889634
CONV_W = 3
LANES = 128
LORA_PAD = 128
VMEM_LIMIT = 56 * 1024 * 1024
NEG_BIG = -1e30
WKV_CHUNK = 64
WKV_HEADS = 4
FEAT = 8


def _params(n_axes):
    return pltpu.CompilerParams(dimension_semantics=("arbitrary",) * n_axes, vmem_limit_bytes=VMEM_LIMIT)


def _rms(x, g):
    return x * lax.rsqrt(jnp.mean(x * x, axis=-1, keepdims=True) + NORM_EPS) * g


def _dot(a, b):
    return jnp.dot(a.astype(BF16), b.astype(BF16), preferred_element_type=F32)


def _dot_nt(a, b):
    return lax.dot_general(a.astype(BF16), b.astype(BF16), (((1,), (1,)), ((), ())), preferred_element_type=F32)


def _dot_tn(a, b):
    return lax.dot_general(a.astype(BF16), b.astype(BF16), (((0,), (0,)), ((), ())), preferred_element_type=F32)


def _sigmoid(x):
    return 1.0 / (1.0 + jnp.exp(-x))


def _softplus(x):
    return jnp.maximum(x, 0.0) + jnp.log(1.0 + jnp.exp(-jnp.abs(x)))


def _split3(x):
    hi = x.astype(BF16)
    r1 = x - hi.astype(F32)
    mid = r1.astype(BF16)
    lo = (r1 - mid.astype(F32)).astype(BF16)
    return hi, mid, lo


def _shift_rows(x, carry, shift):
    n = x.shape[0]
    c = carry.shape[0]
    if shift % 8 == 0:
        return jnp.concatenate([carry[c - shift:], x[:n - shift]], axis=0)
    row = lax.broadcasted_iota(jnp.int32, x.shape, 0)
    out = pltpu.roll(x, shift, axis=0)
    for j in range(shift):
        out = jnp.where(row == j, carry[c - shift + j:c - shift + j + 1], out)
    return out


def _cumsum_chunks(x, c):
    pos = lax.broadcasted_iota(jnp.int32, x.shape, 0) & (c - 1)
    s = 1
    while s < c:
        x = x + jnp.where(pos >= s, pltpu.roll(x, s, axis=0), 0.0)
        s *= 2
    return x


def _const_spec(shape):
    nd = len(shape)
    return pl.BlockSpec(shape, lambda *_: (0,) * nd)


def _pad_cols(w, n):
    return jnp.pad(w, ((0, 0), (0, n - w.shape[1])))


def _pad_rows(w, n):
    return jnp.pad(w, ((0, n - w.shape[0]), (0, 0)))


def _memkv_kernel(mem_ref, gain_ref, w_ref, k_ref, v_ref):
    xn = _rms(mem_ref[0], gain_ref[0])
    kv = _dot(xn, w_ref[0])
    k_ref[0, 0] = kv[:, :MEM_WIDTH]
    v_ref[0, 0] = kv[:, MEM_WIDTH:]


def _mem_kv(mem, mem_norm, w_mem_kv):
    depth, d = mem_norm.shape
    b = mem.shape[0]
    out = jax.ShapeDtypeStruct((depth, b, N_MEM, MEM_WIDTH), F32)
    return pl.pallas_call(
        _memkv_kernel,
        grid=(depth, b),
        in_specs=[pl.BlockSpec((1, N_MEM, d), lambda l, i: (i, 0, 0)),
                  pl.BlockSpec((1, 1, d), lambda l, i: (l, 0, 0)),
                  pl.BlockSpec((1, d, 2 * MEM_WIDTH), lambda l, i: (l, 0, 0))],
        out_specs=[pl.BlockSpec((1, 1, N_MEM, MEM_WIDTH), lambda l, i: (l, i, 0, 0))] * 2,
        out_shape=[out, out],
        compiler_params=_params(2),
        name="mem_kv",
    )(mem, mem_norm.reshape(depth, 1, d), w_mem_kv.astype(BF16))


def _rwkv_in_kernel(*refs, sw, shift, blocks_per_seq, has_vfirst):
    if has_vfirst:
        (h_ref, init_ref, vfirst_ref, gain_ref, mix_ref, w_in_ref, w1_ref, w2_ref, a1_ref, a2_ref, g1_ref, g2_ref,
         v1_ref, v2_ref, v0_ref, vecs_ref,
         r_ref, lw_ref, k_ref, v_ref, kk_ref, a_ref, g_ref, cq_ref, shift_ref, carry_ref) = refs
    else:
        (h_ref, init_ref, gain_ref, mix_ref, w_in_ref, w1_ref, w2_ref, a1_ref, a2_ref, g1_ref, g2_ref, vecs_ref,
         r_ref, lw_ref, k_ref, v_ref, kk_ref, a_ref, g_ref, cq_ref, shift_ref, carry_ref) = refs

    @pl.when(pl.program_id(0) % blocks_per_seq == 0)
    def _():
        carry_ref[...] = init_ref[0]

    xn = _rms(h_ref[...], gain_ref[...])
    tm = xn.shape[0]
    prev = _shift_rows(xn, carry_ref[...], shift)
    carry_ref[...] = xn[tm - shift:]
    shift_ref[0] = xn[tm - shift:]
    xx = prev - xn
    xr, xw, xk, xv, xa, xg = [(xn + xx * mix_ref[j:j + 1]).astype(BF16) for j in range(6)]
    r = _dot(xr, w_in_ref[:, 0:sw])
    k = _dot(xk, w_in_ref[:, sw:2 * sw])
    v = _dot(xv, w_in_ref[:, 2 * sw:3 * sw])
    cq_ref[...] = _dot(xn, w_in_ref[:, 3 * sw:]).astype(BF16)
    w_pre = vecs_ref[0:1] + _dot(jnp.tanh(_dot(xw, w1_ref[...])), w2_ref[...])
    lw_ref[...] = -jnp.exp(-_softplus(-w_pre) - 0.5)
    a = _sigmoid(vecs_ref[1:2] + _dot(_dot(xa, a1_ref[...]), a2_ref[...]))
    if has_vfirst:
        nu = _sigmoid(v0_ref[...] + _dot(_dot(xv, v1_ref[...]), v2_ref[...]))
        v = v + (vfirst_ref[...] - v) * nu
    g_ref[...] = _dot(_sigmoid(_dot(xg, g1_ref[...])), g2_ref[...])
    r_ref[...] = r
    v_ref[...] = v
    a_ref[...] = a
    kk_ref[...] = k * vecs_ref[2:3]
    k_ref[...] = k * (1.0 + (a - 1.0) * vecs_ref[3:4])


def _rwkv_in(h, shift_init, v_first, gain, mix, w_in, w1, w2, a1, a2, g1, g2, v1, v2, v0, vecs, *, tm, shift,
             blocks_per_seq):
    n, d = h.shape
    sw = vecs.shape[1]
    n_seq = shift_init.shape[0]
    has_vfirst = v_first is not None
    row = lambda i: (i, 0)
    seq = lambda i: (i // blocks_per_seq, 0, 0)
    lora = lambda wa, wb: (_pad_cols(wa, LORA_PAD).astype(BF16), _pad_rows(wb, LORA_PAD).astype(BF16))
    w1p, w2p = lora(w1, w2)
    a1p, a2p = lora(a1, a2)
    g1p, g2p = lora(g1, g2)
    ins = [h, shift_init]
    specs = [pl.BlockSpec((tm, d), row), pl.BlockSpec((1, shift, d), seq)]
    if has_vfirst:
        ins.append(v_first)
        specs.append(pl.BlockSpec((tm, sw), row))
    consts = [gain.reshape(1, d), _pad_rows(mix, 8), w_in.astype(BF16), w1p, w2p, a1p, a2p, g1p, g2p]
    if has_vfirst:
        v1p, v2p = lora(v1, v2)
        consts += [v1p, v2p, v0.reshape(1, sw)]
    consts.append(_pad_rows(vecs, 8))
    ins += consts
    specs += [_const_spec(c.shape) for c in consts]
    wide = jax.ShapeDtypeStruct((n, sw), F32)
    outs = [wide] * 7 + [jax.ShapeDtypeStruct((n, MEM_WIDTH), BF16), jax.ShapeDtypeStruct((n_seq, shift, d), F32)]
    out_specs = [pl.BlockSpec((tm, sw), row)] * 7 + [pl.BlockSpec((tm, MEM_WIDTH), row),
                                                     pl.BlockSpec((1, shift, d), seq)]
    return pl.pallas_call(
        functools.partial(_rwkv_in_kernel, sw=sw, shift=shift, blocks_per_seq=blocks_per_seq, has_vfirst=has_vfirst),
        grid=(n // tm,),
        in_specs=specs, out_specs=out_specs, out_shape=outs,
        scratch_shapes=[pltpu.VMEM((shift, d), F32)],
        compiler_params=_params(1),
        name="rwkv_in",
    )(*ins)


def _wkv_kernel(*refs, chunk, n_chunks, heads, has_init):
    if has_init:
        (r_ref, lw_ref, k_ref, v_ref, kk_ref, a_ref, g_ref, vecs_ref, s0_ref, o_ref, sfin_ref, s_scr) = refs
    else:
        (r_ref, lw_ref, k_ref, v_ref, kk_ref, a_ref, g_ref, vecs_ref, o_ref, sfin_ref, s_scr) = refs
    tb = pl.program_id(2)
    c, nc, hd = chunk, n_chunks, HEAD_DIM
    n = c * nc

    @pl.when(tb == 0)
    def _():
        if has_init:
            s_scr[...] = s0_ref[0]
        else:
            s_scr[...] = jnp.zeros_like(s_scr)

    ri = lax.broadcasted_iota(jnp.int32, (n, n), 0)
    ci = lax.broadcasted_iota(jnp.int32, (n, n), 1)
    same = (ri // c) == (ci // c)
    incl = same & (ci <= ri)
    strict = same & (ci < ri)
    eye = (ci == ri).astype(F32)
    own = ((lax.broadcasted_iota(jnp.int32, (n, nc * hd), 0) // c)
           == (lax.broadcasted_iota(jnp.int32, (n, nc * hd), 1) // hd))
    n_double = max(c.bit_length() - 2, 0)

    def expand(x):
        return jnp.where(own, jnp.concatenate([x] * nc, axis=1), 0.0).astype(BF16)

    lw = lw_ref[...]
    cs = _cumsum_chunks(lw, c)
    g_inc_all = jnp.exp(cs)
    g_exc_all = jnp.exp(cs - lw)
    g_inv_all = jnp.exp(-cs)
    r_all, k_all, v_all = r_ref[...], k_ref[...], v_ref[...]
    kk_all, a_all, gate_all = kk_ref[...], a_ref[...], g_ref[...]

    hs = range(heads)
    sls = [slice(h * hd, (h + 1) * hd) for h in hs]
    r_ = [r_all[:, sl] for sl in sls]
    k_ = [k_all[:, sl] for sl in sls]
    v_ = [v_all[:, sl] for sl in sls]
    g_inc = [g_inc_all[:, sl] for sl in sls]
    kk_ = [kk_all[:, sl] for sl in sls]
    kk_ = [x / jnp.maximum(jnp.sqrt(jnp.sum(x * x, axis=-1, keepdims=True)), 1e-12) for x in kk_]
    rt = [r_[h] * g_inc[h] for h in hs]
    at = [-kk_[h] * g_exc_all[:, sls[h]] for h in hs]
    bt = [kk_[h] * a_all[:, sls[h]] * g_inv_all[:, sls[h]] for h in hs]
    kt = [k_[h] * g_inv_all[:, sls[h]] for h in hs]
    bk = [jnp.concatenate([bt[h], kt[h]], axis=0).astype(BF16) for h in hs]
    gram = [_dot_nt(jnp.concatenate([rt[h], at[h]], axis=0), bk[h]) for h in hs]
    m_rb = [jnp.where(incl, x[:n, :n], 0.0).astype(BF16) for x in gram]
    m_rk = [jnp.where(incl, x[:n, n:], 0.0).astype(BF16) for x in gram]
    m_ab = [jnp.where(strict, x[n:, :n], 0.0) for x in gram]
    m_ak = [jnp.where(strict, x[n:, n:], 0.0) for x in gram]
    inv = [eye + x for x in m_ab]
    p = m_ab
    for _ in range(n_double):
        p = [_dot(x, x) for x in p]
        inv = [inv[h] + _dot(inv[h], p[h]) for h in hs]
    inv = [x.astype(BF16) for x in inv]
    vb = [x.astype(BF16) for x in v_]
    w = [_dot(m_ak[h], vb[h]) for h in hs]
    a1 = [_dot(inv[h], at[h]) for h in hs]
    u = [_dot(inv[h], w[h]) for h in hs]
    r1 = [rt[h] + _dot(m_rb[h], a1[h]) for h in hs]
    y0 = [_dot(jnp.concatenate([m_rb[h], m_rk[h]], axis=1), jnp.concatenate([u[h].astype(BF16), vb[h]], axis=0))
          for h in hs]
    phi = [_dot_tn(expand(a1[h]), bt[h]) for h in hs]
    psi = [_dot_tn(jnp.concatenate([expand(u[h]), expand(v_[h])], axis=0), bk[h]) for h in hs]
    s = [s_scr[h] for h in hs]
    ys = [[] for _ in hs]
    for j in range(nc):
        rows = slice(j * c, (j + 1) * c)
        blk = slice(j * hd, (j + 1) * hd)
        for h in hs:
            ys[h].append(_dot_nt(r1[h][rows], s[h]) + y0[h][rows])
        s = [(s[h] + _dot(s[h], phi[h][blk]) + psi[h][blk]) * g_inc[h][(j + 1) * c - 1:(j + 1) * c] for h in hs]
    outs = []
    for h in hs:
        s_scr[h] = s[h]
        sl = sls[h]
        y = jnp.concatenate(ys[h], axis=0) if nc > 1 else ys[h][0]
        mu = jnp.mean(y, axis=-1, keepdims=True)
        var = jnp.mean(jnp.square(y - mu), axis=-1, keepdims=True)
        yn = (y - mu) * lax.rsqrt(var + GN_EPS) * vecs_ref[5:6, sl] + vecs_ref[6:7, sl]
        bonus = jnp.sum(r_[h] * k_[h] * vecs_ref[4:5, sl], axis=-1, keepdims=True) * v_[h]
        outs.append((yn + bonus) * gate_all[:, sl])
    o_ref[...] = jnp.concatenate(outs, axis=-1)

    @pl.when(tb == pl.num_programs(2) - 1)
    def _():
        sfin_ref[0] = s_scr[...]


def _wkv(r, lw, k, v, kk, a, g, vecs, s0, *, n_seq, seq_len, chunk, n_chunks):
    n, sw = r.shape
    n_heads = sw // HEAD_DIM
    width = WKV_HEADS * HEAD_DIM
    tb = chunk * n_chunks
    nb = seq_len // tb
    blk = pl.BlockSpec((tb, width), lambda b, hp, t: (b * nb + t, hp))
    ins = [r, lw, k, v, kk, a, g, _pad_rows(vecs, 8)]
    specs = [blk] * 7 + [pl.BlockSpec((8, width), lambda b, hp, t: (0, hp))]
    state_spec = pl.BlockSpec((1, WKV_HEADS, HEAD_DIM, HEAD_DIM), lambda b, hp, t: (b, hp, 0, 0))
    if s0 is not None:
        ins.append(s0)
        specs.append(state_spec)
    return pl.pallas_call(
        functools.partial(_wkv_kernel, chunk=chunk, n_chunks=n_chunks, heads=WKV_HEADS, has_init=s0 is not None),
        grid=(n_seq, n_heads // WKV_HEADS, nb),
        in_specs=specs,
        out_specs=[blk, state_spec],
        out_shape=[jax.ShapeDtypeStruct((n, sw), F32),
                   jax.ShapeDtypeStruct((n_seq, n_heads, HEAD_DIM, HEAD_DIM), F32)],
        scratch_shapes=[pltpu.VMEM((WKV_HEADS, HEAD_DIM, HEAD_DIM), F32)],
        compiler_params=_params(3),
        name="wkv",
    )(*ins)


def _mem_attn_kernel(q_ref, k_ref, v_ref, o_ref):
    q = q_ref[0]
    k = k_ref[0].astype(BF16)
    v = v_ref[0].astype(BF16)
    outs = []
    for h in range(H_MEM):
        sl = slice(h * HEAD_DIM, (h + 1) * HEAD_DIM)
        s = _dot_nt(q[:, sl], k[:, sl]) * ATTN_SCALE
        p = jnp.exp(s - jnp.max(s, axis=-1, keepdims=True))
        p = p / jnp.sum(p, axis=-1, keepdims=True)
        outs.append(_dot(p, v[:, sl]))
    o_ref[0] = jnp.concatenate(outs, axis=-1).astype(BF16)


def _mem_attn(cq, mem_k, mem_v, *, tq):
    n_seq, t, _ = cq.shape
    qspec = pl.BlockSpec((1, tq, MEM_WIDTH), lambda b, i: (b, i, 0))
    mspec = pl.BlockSpec((1, N_MEM, MEM_WIDTH), lambda b, i: (b, 0, 0))
    return pl.pallas_call(
        _mem_attn_kernel,
        grid=(n_seq, t // tq),
        in_specs=[qspec, mspec, mspec],
        out_specs=qspec,
        out_shape=jax.ShapeDtypeStruct(cq.shape, BF16),
        compiler_params=_params(2),
        name="mem_attn",
    )(cq, mem_k, mem_v)


def _out_proj_kernel(os_ref, om_ref, h_ref, w_ref, gain_ref, o_ref, *, sw):
    mixed = _dot(os_ref[...], w_ref[0:sw, :]) + _dot(om_ref[...], w_ref[sw:, :])
    o_ref[...] = h_ref[...] + _rms(mixed, gain_ref[...])


def _out_proj(o_self, o_mem, h, w_out, gain, *, tm):
    n, d = h.shape
    sw = o_self.shape[1]
    row = lambda i: (i, 0)
    return pl.pallas_call(
        functools.partial(_out_proj_kernel, sw=sw),
        grid=(n // tm,),
        in_specs=[pl.BlockSpec((tm, sw), row), pl.BlockSpec((tm, MEM_WIDTH), row), pl.BlockSpec((tm, d), row),
                  _const_spec(w_out.shape), _const_spec((1, d))],
        out_specs=pl.BlockSpec((tm, d), row),
        out_shape=jax.ShapeDtypeStruct((n, d), F32),
        compiler_params=_params(1),
        name="out_proj",
    )(o_self, o_mem, h, w_out.astype(BF16), gain.reshape(1, d))


def _gelu_tanh(x):
    return 0.5 * x * (1.0 + jnp.tanh(0.7978845608028654 * (x + 0.044715 * (x * x * x))))


def _ffn_kernel(h_ref, init_ref, gin_ref, gout_ref, wup_ref, cw_ref, cb_ref, wdn_ref, o_ref, conv_ref, carry_ref,
                acc_ref, *, d_ff, shift, blocks_per_seq, ff_chunk):
    @pl.when(pl.program_id(0) % blocks_per_seq == 0)
    def _():
        carry_ref[...] = init_ref[0]

    h = h_ref[...]
    tm = h.shape[0]
    xn = _rms(h, gin_ref[...]).astype(BF16)
    for j in range(d_ff // ff_chunk):
        cols = slice(j * ff_chunk, (j + 1) * ff_chunk)
        u = _dot(xn, wup_ref[:, cols])
        g = _dot(xn, wup_ref[:, d_ff + j * ff_chunk:d_ff + (j + 1) * ff_chunk])
        carry = carry_ref[:, cols]
        g1 = _shift_rows(g, carry, shift)
        g2 = _shift_rows(g1, carry[:shift], shift)
        gc = cb_ref[:, cols] + g2 * cw_ref[0:1, cols] + g1 * cw_ref[1:2, cols] + g * cw_ref[2:3, cols]
        carry_ref[:, cols] = g[tm - 2 * shift:]
        conv_ref[0, :, cols] = g[tm - 2 * shift:]
        part = _dot(_gelu_tanh(gc) * u, wdn_ref[cols, :])
        if j == 0:
            acc_ref[...] = part
        else:
            acc_ref[...] += part
    o_ref[...] = h + _rms(acc_ref[...], gout_ref[...])


def _ffn(h, conv_init, gain_in, gain_out, w_up, conv_w, conv_b, w_down, *, tm, shift, blocks_per_seq):
    n, d = h.shape
    d_ff = w_down.shape[0]
    n_seq = conv_init.shape[0]
    row = lambda i: (i, 0)
    seq = lambda i: (i // blocks_per_seq, 0, 0)
    once = pl.Buffered(1)
    return pl.pallas_call(
        functools.partial(_ffn_kernel, d_ff=d_ff, shift=shift, blocks_per_seq=blocks_per_seq, ff_chunk=2 * LANES),
        grid=(n // tm,),
        in_specs=[pl.BlockSpec((tm, d), row), pl.BlockSpec((1, 2 * shift, d_ff), seq),
                  _const_spec((1, d)), _const_spec((1, d)),
                  pl.BlockSpec(w_up.shape, lambda i: (0, 0), pipeline_mode=once),
                  _const_spec((8, d_ff)), _const_spec((1, d_ff)),
                  pl.BlockSpec(w_down.shape, lambda i: (0, 0), pipeline_mode=once)],
        out_specs=[pl.BlockSpec((tm, d), row), pl.BlockSpec((1, 2 * shift, d_ff), seq)],
        out_shape=[jax.ShapeDtypeStruct((n, d), F32), jax.ShapeDtypeStruct((n_seq, 2 * shift, d_ff), F32)],
        scratch_shapes=[pltpu.VMEM((2 * shift, d_ff), F32), pltpu.VMEM((tm, d), F32)],
        compiler_params=_params(1),
        name="ffn",
    )(h, conv_init, gain_in.reshape(1, d), gain_out.reshape(1, d), w_up.astype(BF16), _pad_rows(conv_w, 8),
      conv_b.reshape(1, d_ff), w_down.astype(BF16))


def _head_mean_sq(x, ones_bd):
    outs = []
    for j in range(x.shape[1] // 256):
        sq = jnp.square(x[:, j * 256:(j + 1) * 256])
        hi = sq.astype(BF16)
        lo = (sq - hi.astype(F32)).astype(BF16)
        outs.append(jnp.dot(hi, ones_bd, preferred_element_type=F32) + jnp.dot(lo, ones_bd, preferred_element_type=F32))
    return jnp.concatenate(outs, axis=-1) * (1.0 / HEAD_DIM)


def _feat_base(h):
    return FEAT * (h // 2) + HEAD_DIM * (1 - h % 2)


def _bias_features(c, n_heads):
    terms = jnp.concatenate(_split3(c * LOG2E), axis=1)
    r = lax.broadcasted_iota(jnp.int32, (3 * LANES, LANES), 0)
    l = lax.broadcasted_iota(jnp.int32, (3 * LANES, LANES), 1)
    head = r % LANES
    off = l - _feat_base(head)
    in_head = (off >= 0) & (off < FEAT) & (head < n_heads)
    place_q = (in_head & (off == r // LANES)).astype(BF16)
    place_k = (in_head & (off == 3 + r // LANES)).astype(BF16)
    lane = lax.broadcasted_iota(jnp.int32, (1, LANES), 1)
    slot = lane % FEAT
    used = ((lane % HEAD_DIM) // FEAT) < (n_heads // 2)
    ones_q = (used & (slot >= 3) & (slot < 6)).astype(F32)
    ones_k = (used & (slot < 3)).astype(F32)
    qf = jnp.dot(terms, place_q, preferred_element_type=F32) + ones_q
    kf = ones_k - jnp.dot(terms, place_k, preferred_element_type=F32)
    return qf.astype(BF16), kf.astype(BF16)


def _fox_in_kernel(h_ref, gain_ref, w_ref, bf_ref, qkg_ref, q_ref, k_ref, v_ref, kb_ref, vb_ref, lf_ref, c_ref,
                   qf_ref, kf_ref, og_ref, cq_ref, carry_ref, *, sw, n_heads, shift, steps, blocks_per_seq):
    xn = _rms(h_ref[...], gain_ref[...]).astype(BF16)
    tm = xn.shape[0]
    ri = lax.broadcasted_iota(jnp.int32, (256, 256), 0) // HEAD_DIM
    ci = lax.broadcasted_iota(jnp.int32, (256, 256), 1) // HEAD_DIM
    ones_bd = (ri == ci).astype(BF16)
    q = jnp.dot(xn, w_ref[:, 0:sw], preferred_element_type=F32)
    q = q * lax.rsqrt(_head_mean_sq(q, ones_bd) + NORM_EPS) * qkg_ref[0:1]
    q_ref[...] = (q * (ATTN_SCALE * LOG2E)).astype(BF16)
    k = jnp.dot(xn, w_ref[:, sw:2 * sw], preferred_element_type=F32)
    k = k * lax.rsqrt(_head_mean_sq(k, ones_bd) + NORM_EPS) * qkg_ref[1:2]
    k_ref[...] = k
    kb_ref[...] = k.astype(BF16)
    v = jnp.dot(xn, w_ref[:, 2 * sw:3 * sw], preferred_element_type=F32)
    v_ref[...] = v
    vb_ref[...] = v.astype(BF16)
    o0 = 3 * sw + LANES
    lf = -_softplus(-(jnp.dot(xn, w_ref[:, 3 * sw:o0], preferred_element_type=F32) + bf_ref[...]))
    lf_ref[...] = lf
    og_ref[...] = _sigmoid(jnp.dot(xn, w_ref[:, o0:o0 + sw], preferred_element_type=F32))
    cq_ref[...] = jnp.dot(xn, w_ref[:, o0 + sw:], preferred_element_type=F32).astype(BF16)
    if steps is None:
        @pl.when(pl.program_id(0) % blocks_per_seq == 0)
        def _():
            carry_ref[...] = jnp.zeros_like(carry_ref)

        c = _cumsum_chunks(lf, tm) + carry_ref[0:1]
        carry_ref[...] = jnp.broadcast_to(c[tm - 1:tm], carry_ref.shape)
    else:
        parts = [lf[0:shift]]
        for t in range(1, steps):
            parts.append(parts[-1] + lf[t * shift:(t + 1) * shift])
        c = jnp.concatenate(parts, axis=0)
    c_ref[...] = c
    qf, kf = _bias_features(c, n_heads)
    qf_ref[...] = qf
    kf_ref[...] = kf


def _fox_in(h, gain, w_in, bf, qk_gain, *, tm, shift, steps, blocks_per_seq):
    n, d = h.shape
    n_heads = bf.shape[0]
    sw = n_heads * HEAD_DIM
    w = jnp.concatenate([w_in[:, :3 * sw], _pad_cols(w_in[:, 3 * sw:3 * sw + n_heads], LANES),
                         w_in[:, 3 * sw + n_heads:]], axis=1).astype(BF16)
    qkg = _pad_rows(jnp.tile(qk_gain, (1, n_heads)), 8)
    row = lambda i: (i, 0)
    wide = pl.BlockSpec((tm, sw), row)
    lane = pl.BlockSpec((tm, LANES), row)
    wide_f32 = jax.ShapeDtypeStruct((n, sw), F32)
    wide_bf16 = jax.ShapeDtypeStruct((n, sw), BF16)
    lane_f32 = jax.ShapeDtypeStruct((n, LANES), F32)
    lane_bf16 = jax.ShapeDtypeStruct((n, LANES), BF16)
    return pl.pallas_call(
        functools.partial(_fox_in_kernel, sw=sw, n_heads=n_heads, shift=shift, steps=steps,
                          blocks_per_seq=blocks_per_seq),
        grid=(n // tm,),
        in_specs=[pl.BlockSpec((tm, d), row), _const_spec((1, d)), _const_spec(w.shape), _const_spec((1, LANES)),
                  _const_spec((8, sw))],
        out_specs=[wide, wide, wide, wide, wide, lane, lane, lane, lane, wide, pl.BlockSpec((tm, MEM_WIDTH), row)],
        out_shape=[wide_bf16, wide_f32, wide_f32, wide_bf16, wide_bf16, lane_f32, lane_f32, lane_bf16, lane_bf16,
                   wide_f32, jax.ShapeDtypeStruct((n, MEM_WIDTH), BF16)],
        scratch_shapes=[pltpu.VMEM((8, LANES), F32)],
        compiler_params=_params(1),
        name="fox_in",
    )(h, gain.reshape(1, d), w, _pad_cols(bf.reshape(1, n_heads), LANES), qkg)


def _fox_flash_kernel(q_ref, k_ref, v_ref, qf_ref, kf_ref, og_ref, o_ref, m_ref, l_ref, acc_ref, *, n_heads, tq, tk):
    qi = pl.program_id(1)
    ki = pl.program_id(2)
    last = ((qi + 1) * tq - 1) // tk

    @pl.when(ki == 0)
    def _():
        m_ref[...] = jnp.full_like(m_ref, NEG_BIG)
        l_ref[...] = jnp.zeros_like(l_ref)
        acc_ref[...] = jnp.zeros_like(acc_ref)

    lane = lax.broadcasted_iota(jnp.int32, (1, LANES), 1)
    upper = lane >= HEAD_DIM
    zero = jnp.zeros((), BF16)

    def step(masked):
        if masked:
            qpos = qi * tq + lax.broadcasted_iota(jnp.int32, (tq, tk), 0)
            kpos = ki * tk + lax.broadcasted_iota(jnp.int32, (tq, tk), 1)
            causal = kpos <= qpos
        qf = qf_ref[...]
        kf = kf_ref[...]
        for pr in range(n_heads // 2):
            cols = slice(pr * LANES, (pr + 1) * LANES)
            heads = (2 * pr, 2 * pr + 1)
            q2, k2, v2 = q_ref[:, cols], k_ref[:, cols], v_ref[:, cols]
            own = [upper == bool(h % 2) for h in heads]
            feat = [(lane >= _feat_base(h)) & (lane < _feat_base(h) + FEAT) for h in heads]
            q_aug = [jnp.where(own[i], q2, jnp.where(feat[i], qf, zero)) for i in range(2)]
            k_aug = [jnp.where(own[i], k2, kf) for i in range(2)]
            s = [_dot_nt(q_aug[i], k_aug[i]) for i in range(2)]
            if masked:
                s = [jnp.where(causal, x, NEG_BIG) for x in s]
            m_old = [m_ref[:, h:h + 1] for h in heads]
            m_new = [jnp.maximum(m_old[i], jnp.max(s[i], axis=-1, keepdims=True)) for i in range(2)]
            alpha = [jnp.exp2(m_old[i] - m_new[i]) for i in range(2)]
            p = [jnp.exp2(s[i] - m_new[i]) for i in range(2)]
            for i, h in enumerate(heads):
                l_ref[:, h:h + 1] = alpha[i] * l_ref[:, h:h + 1] + jnp.sum(p[i], axis=-1, keepdims=True)
                m_ref[:, h:h + 1] = m_new[i]
            pv = [jnp.dot(p[i].astype(BF16), v2, preferred_element_type=F32) for i in range(2)]
            acc = acc_ref[:, cols]
            acc_ref[:, cols] = jnp.where(upper, alpha[1] * acc + pv[1], alpha[0] * acc + pv[0])

    @pl.when(ki < last)
    def _():
        step(False)

    @pl.when(ki == last)
    def _():
        step(True)
        for pr in range(n_heads // 2):
            cols = slice(pr * LANES, (pr + 1) * LANES)
            inv_l = jnp.where(upper, 1.0 / l_ref[:, 2 * pr + 1:2 * pr + 2], 1.0 / l_ref[:, 2 * pr:2 * pr + 1])
            o_ref[:, cols] = acc_ref[:, cols] * inv_l * og_ref[:, cols]


def _fox_flash(q, k, v, qf, kf, og, *, n_seq, seq_len, tq, tk):
    n, sw = k.shape
    n_heads = sw // HEAD_DIM
    nq, nk = seq_len // tq, seq_len // tk
    qrow = lambda b, i, j: (b * nq + i, 0)
    krow = lambda b, i, j: (b * nk + jnp.minimum(j, ((i + 1) * tq - 1) // tk), 0)
    return pl.pallas_call(
        functools.partial(_fox_flash_kernel, n_heads=n_heads, tq=tq, tk=tk),
        grid=(n_seq, nq, nk),
        in_specs=[pl.BlockSpec((tq, sw), qrow), pl.BlockSpec((tk, sw), krow), pl.BlockSpec((tk, sw), krow),
                  pl.BlockSpec((tq, LANES), qrow), pl.BlockSpec((tk, LANES), krow), pl.BlockSpec((tq, sw), qrow)],
        out_specs=pl.BlockSpec((tq, sw), qrow),
        out_shape=jax.ShapeDtypeStruct((n, sw), F32),
        scratch_shapes=[pltpu.VMEM((tq, LANES), F32), pltpu.VMEM((tq, LANES), F32), pltpu.VMEM((tq, sw), F32)],
        compiler_params=_params(3),
        name="fox_flash",
    )(q, k, v, qf, kf, og)


HEAD_ROWS = 16
PAGE_GROUP = 8


def _fox_paged_kernel(pt_ref, q_ref, cn_ref, kn_ref, vn_ref, og_ref, k_hbm, v_hbm, lp_hbm, o_ref,
                      kbuf, vbuf, lpbuf, sem, m_ref, l_ref, acc_ref, suf_ref, *, layer, n_heads, n_pages, group, page):
    i = pl.program_id(0)
    j = pl.program_id(1)
    n_groups = pl.num_programs(1)
    g = i * n_groups + j
    hd = HEAD_DIM
    hs = range(n_heads)

    def copies(seq, grp, slot):
        out = []
        for u in range(group):
            pg = pt_ref[seq, n_pages - 1 - (grp * group + u)]
            for h in hs:
                out.append(pltpu.make_async_copy(k_hbm.at[layer, pg, :, h, :], kbuf.at[slot, u, h], sem.at[slot, 0]))
                out.append(pltpu.make_async_copy(v_hbm.at[layer, pg, :, h, :], vbuf.at[slot, u, h], sem.at[slot, 1]))
            out.append(pltpu.make_async_copy(lp_hbm.at[layer, pg], lpbuf.at[slot, u], sem.at[slot, 2]))
        return out

    @pl.when(g == 0)
    def _():
        for cp in copies(0, 0, 0):
            cp.start()

    @pl.when(g + 1 < pl.num_programs(0) * n_groups)
    def _():
        wrap = j + 1 == n_groups
        for cp in copies(jnp.where(wrap, i + 1, i), jnp.where(wrap, 0, j + 1), (g + 1) % 2):
            cp.start()

    slot = g % 2
    q = q_ref[0]
    q_ = [q[:, h * hd:(h + 1) * hd] for h in hs]

    @pl.when(j == 0)
    def _():
        kn, vn = kn_ref[0], vn_ref[0]
        steps = q.shape[0]
        trow = lax.broadcasted_iota(jnp.int32, (steps, steps), 0)
        tcol = lax.broadcasted_iota(jnp.int32, (steps, steps), 1)
        s = [jnp.where(tcol <= trow,
                       _dot_nt(q_[h], kn[:, h * hd:(h + 1) * hd]) + LOG2E * cn_ref[0, h][:, 0:steps], NEG_BIG)
             for h in hs]
        m = [jnp.max(x, axis=-1, keepdims=True) for x in s]
        p = [jnp.exp2(s[h] - m[h]) for h in hs]
        for h in hs:
            m_ref[h] = m[h]
            l_ref[h] = jnp.sum(p[h], axis=-1, keepdims=True)
            acc_ref[h] = _dot(p[h], vn[:, h * hd:(h + 1) * hd])
        suf_ref[...] = jnp.zeros_like(suf_ref)

    for cp in copies(i, j, slot):
        cp.wait()

    lp = [lpbuf[slot, u] for u in range(group)]
    ri = lax.broadcasted_iota(jnp.int32, (page, page), 0)
    ci = lax.broadcasted_iota(jnp.int32, (page, page), 1)
    later = (ri > ci).astype(BF16)
    hi, mid, lo = _split3(jnp.concatenate(lp, axis=0))
    within = (jnp.dot(hi, later, preferred_element_type=F32) + jnp.dot(mid, later, preferred_element_type=F32)
              + jnp.dot(lo, later, preferred_element_type=F32))
    carry = [suf_ref[...]]
    for u in range(group):
        carry.append(carry[-1] + jnp.sum(lp[u], axis=-1, keepdims=True))
    suf_ref[...] = carry[group]
    n_self = q.shape[0]
    bias = [jnp.concatenate([within[u * HEAD_ROWS + h:u * HEAD_ROWS + h + 1] + carry[u][h:h + 1]
                             for u in range(group)], axis=1) for h in hs]
    kcat = [jnp.concatenate([kbuf[slot, u, h] for u in range(group)], axis=0).astype(BF16) for h in hs]
    vcat = [jnp.concatenate([vbuf[slot, u, h] for u in range(group)], axis=0).astype(BF16) for h in hs]
    s = [_dot_nt(q_[h], kcat[h]) + LOG2E * (bias[h] + cn_ref[0, h][:, n_self:n_self + 1]) for h in hs]
    m_old = [m_ref[h] for h in hs]
    m_new = [jnp.maximum(m_old[h], jnp.max(s[h], axis=-1, keepdims=True)) for h in hs]
    alpha = [jnp.exp2(m_old[h] - m_new[h]) for h in hs]
    p = [jnp.exp2(s[h] - m_new[h]) for h in hs]
    pv = [_dot(p[h], vcat[h]) for h in hs]
    for h in hs:
        l_ref[h] = alpha[h] * l_ref[h] + jnp.sum(p[h], axis=-1, keepdims=True)
        acc_ref[h] = alpha[h] * acc_ref[h] + pv[h]
        m_ref[h] = m_new[h]

    @pl.when(j == n_groups - 1)
    def _():
        o_ref[0] = jnp.concatenate([acc_ref[h] / l_ref[h] for h in hs], axis=-1) * og_ref[0]


def _fox_paged(page_table, q, cn, k_new, v_new, og, cache_k, cache_v, cache_lp_t, *, layer, group):
    b, steps, sw = q.shape
    n_heads = sw // HEAD_DIM
    n_pages = page_table.shape[1]
    page = cache_k.shape[2]
    seq3 = lambda i, j, pt: (i, 0, 0)
    seq4 = lambda i, j, pt: (i, 0, 0, 0)
    tok = pl.BlockSpec((1, steps, sw), seq3)
    hbm = pl.BlockSpec(memory_space=pl.ANY)
    grid_spec = pltpu.PrefetchScalarGridSpec(
        num_scalar_prefetch=1,
        grid=(b, n_pages // group),
        in_specs=[tok, pl.BlockSpec((1, n_heads, steps, 2 * steps), seq4), tok, tok, tok, hbm, hbm, hbm],
        out_specs=tok,
        scratch_shapes=[pltpu.VMEM((2, group, n_heads, page, HEAD_DIM), F32),
                        pltpu.VMEM((2, group, n_heads, page, HEAD_DIM), F32),
                        pltpu.VMEM((2, group, HEAD_ROWS, page), F32),
                        pltpu.SemaphoreType.DMA((2, 3)),
                        pltpu.VMEM((n_heads, steps, 1), F32), pltpu.VMEM((n_heads, steps, 1), F32),
                        pltpu.VMEM((n_heads, steps, HEAD_DIM), F32), pltpu.VMEM((HEAD_ROWS, 1), F32)],
    )
    return pl.pallas_call(
        functools.partial(_fox_paged_kernel, layer=layer, n_heads=n_heads, n_pages=n_pages, group=group, page=page),
        grid_spec=grid_spec,
        out_shape=jax.ShapeDtypeStruct((b, steps, sw), F32),
        compiler_params=_params(2),
        name="fox_paged",
    )(page_table, q, cn, k_new, v_new, og, cache_k, cache_v, cache_lp_t)


def _to_time_major(x):
    x = jnp.swapaxes(x, 0, 1)
    return x.reshape((x.shape[0] * x.shape[1],) + x.shape[2:])


def _to_batch_major(x, b):
    return jnp.swapaxes(x.reshape((x.shape[0] // b, b) + x.shape[1:]), 0, 1)


def _trunk(h, P, *, prompt, n_seq, seq_len, mem_k, mem_v, rw_shift, rw_wkv, ffn_conv, fox_cache):
    depth = P["norm_gains"].shape[0]
    n, d = h.shape
    sw = P["rw_vecs"].shape[2]
    n_heads = sw // HEAD_DIM
    if prompt:
        tm, shift, blocks_per_seq = 256, 1, seq_len // 256
    else:
        tm, shift, blocks_per_seq = n, n_seq, 1
    v_first = None
    shifts, wkvs, fks, fvs, fls, convs = [], [], [], [], [], []
    for l in range(depth):
        gains = P["norm_gains"][l]
        i = l // 2
        if l % 2 == 0:
            init = jnp.zeros((n_seq, 1, d), F32) if prompt else rw_shift[i][None]
            vj = None if i == 0 else i - 1
            r, lw, k, v, kk, a, g, cq, sh = _rwkv_in(
                h, init, v_first, gains[0], P["rw_mix"][i], P["rw_w_in"][i], P["rw_w1"][i], P["rw_w2"][i],
                P["rw_a1"][i], P["rw_a2"][i], P["rw_g1"][i], P["rw_g2"][i],
                None if vj is None else P["rw_v1"][vj], None if vj is None else P["rw_v2"][vj],
                None if vj is None else P["rw_v0"][vj], P["rw_vecs"][i],
                tm=tm, shift=shift, blocks_per_seq=blocks_per_seq)
            if v_first is None:
                v_first = v
            if prompt:
                o_self, s_fin = _wkv(r, lw, k, v, kk, a, g, P["rw_vecs"][i], None, n_seq=n_seq, seq_len=seq_len,
                                     chunk=WKV_CHUNK, n_chunks=4)
                shifts.append(sh[:, 0])
            else:
                pad = lambda x: jnp.pad(_to_batch_major(x, n_seq), ((0, 0), (0, 8 - seq_len), (0, 0))).reshape(n_seq * 8, sw)
                o_pad, s_fin = _wkv(*[pad(x) for x in (r, lw, k, v, kk, a, g)], P["rw_vecs"][i], rw_wkv[i],
                                    n_seq=n_seq, seq_len=8, chunk=8, n_chunks=1)
                o_self = _to_time_major(o_pad.reshape(n_seq, 8, sw)[:, :seq_len])
                shifts.append(sh[0])
            wkvs.append(s_fin)
        else:
            q, k, v, kb, vb, lf, c, qf, kf, og, cq = _fox_in(
                h, gains[0], P["fox_w_in"][i], P["fox_bf"][i], P["fox_qk_gain"][i],
                tm=tm, shift=shift, steps=None if prompt else seq_len, blocks_per_seq=blocks_per_seq)
            if prompt:
                o_self = _fox_flash(q, kb, vb, qf, kf, og, n_seq=n_seq, seq_len=seq_len, tq=128,
                                    tk=min(1024, seq_len))
            else:
                cache_k, cache_v, cache_lp_t, page_table = fox_cache
                pad8 = lambda x: jnp.pad(_to_batch_major(x, n_seq), ((0, 0), (0, 8 - seq_len), (0, 0)))
                ch = jnp.swapaxes(pad8(c)[:, :, :n_heads], 1, 2)
                cn = jnp.concatenate([ch[:, :, :, None] - ch[:, :, None, :],
                                      jnp.broadcast_to(ch[..., None], ch.shape + (8,))], axis=-1)
                o_b = _fox_paged(page_table, pad8(q), cn, pad8(k), pad8(v), pad8(og), cache_k, cache_v, cache_lp_t,
                                 layer=i, group=min(PAGE_GROUP, page_table.shape[1]))
                o_self = _to_time_major(o_b[:, :seq_len])
            fks.append(k)
            fvs.append(v)
            fls.append(lf[:, :n_heads])
        if prompt:
            o_mem = _mem_attn(cq.reshape(n_seq, seq_len, MEM_WIDTH), mem_k[l], mem_v[l],
                              tq=min(1024, seq_len)).reshape(n, MEM_WIDTH)
        else:
            cq_b = jnp.pad(_to_batch_major(cq, n_seq), ((0, 0), (0, 16 - seq_len), (0, 0)))
            o_mem = _to_time_major(_mem_attn(cq_b, mem_k[l], mem_v[l], tq=16)[:, :seq_len])
        h = _out_proj(o_self, o_mem, h, P["w_out"][l], gains[1], tm=tm)
        if prompt:
            conv_init = jnp.zeros((n_seq, CONV_W - 1, P["ffn_conv_b"].shape[1]), F32)
        else:
            conv_init = _to_time_major(ffn_conv[l])[None]
        h, cs = _ffn(h, conv_init, gains[2], gains[3], P["ffn_w_up"][l], P["ffn_conv_w"][l], P["ffn_conv_b"][l],
                     P["ffn_w_down"][l], tm=tm, shift=shift, blocks_per_seq=blocks_per_seq)
        convs.append(cs if prompt else _to_batch_major(cs[0], n_seq))
    return h, wkvs, shifts, fks, fvs, fls, convs


def kernel(x_prompt, x_sample, cache_mem_k, cache_mem_v, cache_fox_k, cache_fox_v, cache_fox_logf, state_rwkv_wkv,
           state_rwkv_shift, state_ffn_conv, page_table, mem_prompt, norm_gains, w_out, mem_norm, w_mem_kv, rw_mix,
           rw_w_in, rw_vecs, rw_w1, rw_w2, rw_a1, rw_a2, rw_g1, rw_g2, rw_v0, rw_v1, rw_v2, fox_w_in, fox_bf,
           fox_qk_gain, ffn_w_up, ffn_conv_w, ffn_conv_b, ffn_w_down):
    P = dict(norm_gains=norm_gains, w_out=w_out, rw_mix=rw_mix, rw_w_in=rw_w_in, rw_vecs=rw_vecs, rw_w1=rw_w1,
             rw_w2=rw_w2, rw_a1=rw_a1, rw_a2=rw_a2, rw_g1=rw_g1, rw_g2=rw_g2, rw_v0=rw_v0, rw_v1=rw_v1, rw_v2=rw_v2,
             fox_w_in=fox_w_in, fox_bf=fox_bf, fox_qk_gain=fox_qk_gain, ffn_w_up=ffn_w_up, ffn_conv_w=ffn_conv_w,
             ffn_conv_b=ffn_conv_b, ffn_w_down=ffn_w_down)
    b, s, d = x_prompt.shape
    db, ds, _ = x_sample.shape
    depth = norm_gains.shape[0]
    n_heads = fox_bf.shape[1]
    sw = n_heads * HEAD_DIM

    p_mem_k, p_mem_v = _mem_kv(mem_prompt, mem_norm, w_mem_kv)
    hp, p_wkv, p_shift, p_fk, p_fv, p_fl, p_conv = _trunk(
        x_prompt.reshape(b * s, d), P, prompt=True, n_seq=b, seq_len=s, mem_k=p_mem_k, mem_v=p_mem_v,
        rw_shift=None, rw_wkv=None, ffn_conv=None, fox_cache=None)

    cache_lp_t = jnp.pad(jnp.swapaxes(cache_fox_logf, 2, 3), ((0, 0), (0, 0), (0, HEAD_ROWS - n_heads), (0, 0)))
    hs, s_wkv, s_shift, s_fk, s_fv, s_fl, s_conv = _trunk(
        _to_time_major(x_sample), P, prompt=False, n_seq=db, seq_len=ds,
        mem_k=cache_mem_k.reshape(depth, db, N_MEM, MEM_WIDTH), mem_v=cache_mem_v.reshape(depth, db, N_MEM, MEM_WIDTH),
        rw_shift=state_rwkv_shift, rw_wkv=state_rwkv_wkv, ffn_conv=state_ffn_conv,
        fox_cache=(cache_fox_k, cache_fox_v, cache_lp_t, page_table))

    heads = lambda x, bb, tt: x.reshape(bb, tt, n_heads, HEAD_DIM)
    mem_heads = lambda x: x.reshape(depth, b, N_MEM, H_MEM, HEAD_DIM)
    return (hp.reshape(b, s, d), _to_batch_major(hs, db),
            jnp.stack(p_wkv), jnp.stack(p_shift),
            jnp.stack([heads(x, b, s) for x in p_fk]), jnp.stack([heads(x, b, s) for x in p_fv]),
            jnp.stack([x.reshape(b, s, n_heads) for x in p_fl]),
            mem_heads(p_mem_k), mem_heads(p_mem_v), jnp.stack(p_conv),
            jnp.stack(s_wkv), jnp.stack(s_shift),
            jnp.stack([heads(_to_batch_major(x, db), db, ds) for x in s_fk]),
            jnp.stack([heads(_to_batch_major(x, db), db, ds) for x in s_fv]),
            jnp.stack([_to_batch_major(x, db) for x in s_fl]),
            jnp.stack(s_conv))
```

```python
import functools

import jax
import jax.numpy as jnp
from jax import lax
from jax.experimental import pallas as pl
from jax.experimental.pallas import tpu as pltpu

F32 = jnp.float32
BF16 = jnp.bfloat16

HEAD_DIM = 64
N_MEM = 256
H_MEM = 4
MEM_WIDTH = H_MEM * HEAD_DIM
NORM_EPS = 1e-6
GN_EPS = 64e-5
ATTN_SCALE = HEAD_DIM ** -0.5
LOG2E = 1.4426950408889634
CONV_W = 3
LANES = 128
LORA_PAD = 128
VMEM_LIMIT = 56 * 1024 * 1024
NEG_BIG = -1e30
WKV_CHUNK = 64
WKV_HEADS = 4
FEAT = 8


def _params(n_axes):
    return pltpu.CompilerParams(dimension_semantics=("arbitrary",) * n_axes, vmem_limit_bytes=VMEM_LIMIT)


def _rms(x, g):
    return x * lax.rsqrt(jnp.mean(x * x, axis=-1, keepdims=True) + NORM_EPS) * g


def _dot(a, b):
    return jnp.dot(a.astype(BF16), b.astype(BF16), preferred_element_type=F32)


def _dot_nt(a, b):
    return lax.dot_general(a.astype(BF16), b.astype(BF16), (((1,), (1,)), ((), ())), preferred_element_type=F32)


def _dot_tn(a, b):
    return lax.dot_general(a.astype(BF16), b.astype(BF16), (((0,), (0,)), ((), ())), preferred_element_type=F32)


def _sigmoid(x):
    return 1.0 / (1.0 + jnp.exp(-x))


def _softplus(x):
    return jnp.maximum(x, 0.0) + jnp.log(1.0 + jnp.exp(-jnp.abs(x)))


def _split3(x):
    hi = x.astype(BF16)
    r1 = x - hi.astype(F32)
    mid = r1.astype(BF16)
    lo = (r1 - mid.astype(F32)).astype(BF16)
    return hi, mid, lo


def _shift_rows(x, carry, shift):
    n = x.shape[0]
    c = carry.shape[0]
    if shift % 8 == 0:
        return jnp.concatenate([carry[c - shift:], x[:n - shift]], axis=0)
    row = lax.broadcasted_iota(jnp.int32, x.shape, 0)
    out = pltpu.roll(x, shift, axis=0)
    for j in range(shift):
        out = jnp.where(row == j, carry[c - shift + j:c - shift + j + 1], out)
    return out


def _cumsum_chunks(x, c):
    pos = lax.broadcasted_iota(jnp.int32, x.shape, 0) & (c - 1)
    s = 1
    while s < c:
        x = x + jnp.where(pos >= s, pltpu.roll(x, s, axis=0), 0.0)
        s *= 2
    return x


def _const_spec(shape):
    nd = len(shape)
    return pl.BlockSpec(shape, lambda *_: (0,) * nd)


def _pad_cols(w, n):
    return jnp.pad(w, ((0, 0), (0, n - w.shape[1])))


def _pad_rows(w, n):
    return jnp.pad(w, ((0, n - w.shape[0]), (0, 0)))


def _memkv_kernel(mem_ref, gain_ref, w_ref, k_ref, v_ref):
    xn = _rms(mem_ref[0], gain_ref[0])
    kv = _dot(xn, w_ref[0])
    k_ref[0, 0] = kv[:, :MEM_WIDTH]
    v_ref[0, 0] = kv[:, MEM_WIDTH:]


def _mem_kv(mem, mem_norm, w_mem_kv):
    depth, d = mem_norm.shape
    b = mem.shape[0]
    out = jax.ShapeDtypeStruct((depth, b, N_MEM, MEM_WIDTH), F32)
    return pl.pallas_call(
        _memkv_kernel,
        grid=(depth, b),
        in_specs=[pl.BlockSpec((1, N_MEM, d), lambda l, i: (i, 0, 0)),
                  pl.BlockSpec((1, 1, d), lambda l, i: (l, 0, 0)),
                  pl.BlockSpec((1, d, 2 * MEM_WIDTH), lambda l, i: (l, 0, 0))],
        out_specs=[pl.BlockSpec((1, 1, N_MEM, MEM_WIDTH), lambda l, i: (l, i, 0, 0))] * 2,
        out_shape=[out, out],
        compiler_params=_params(2),
        name="mem_kv",
    )(mem, mem_norm.reshape(depth, 1, d), w_mem_kv.astype(BF16))


def _rwkv_in_kernel(*refs, sw, shift, blocks_per_seq, has_vfirst):
    if has_vfirst:
        (h_ref, init_ref, vfirst_ref, gain_ref, mix_ref, w_in_ref, w1_ref, w2_ref, a1_ref, a2_ref, g1_ref, g2_ref,
         v1_ref, v2_ref, v0_ref, vecs_ref,
         r_ref, lw_ref, k_ref, v_ref, kk_ref, a_ref, g_ref, cq_ref, shift_ref, carry_ref) = refs
    else:
        (h_ref, init_ref, gain_ref, mix_ref, w_in_ref, w1_ref, w2_ref, a1_ref, a2_ref, g1_ref, g2_ref, vecs_ref,
         r_ref, lw_ref, k_ref, v_ref, kk_ref, a_ref, g_ref, cq_ref, shift_ref, carry_ref) = refs

    @pl.when(pl.program_id(0) % blocks_per_seq == 0)
    def _():
        carry_ref[...] = init_ref[0]

    xn = _rms(h_ref[...], gain_ref[...])
    tm = xn.shape[0]
    prev = _shift_rows(xn, carry_ref[...], shift)
    carry_ref[...] = xn[tm - shift:]
    shift_ref[0] = xn[tm - shift:]
    xx = prev - xn
    xr, xw, xk, xv, xa, xg = [(xn + xx * mix_ref[j:j + 1]).astype(BF16) for j in range(6)]
    r = _dot(xr, w_in_ref[:, 0:sw])
    k = _dot(xk, w_in_ref[:, sw:2 * sw])
    v = _dot(xv, w_in_ref[:, 2 * sw:3 * sw])
    cq_ref[...] = _dot(xn, w_in_ref[:, 3 * sw:]).astype(BF16)
    w_pre = vecs_ref[0:1] + _dot(jnp.tanh(_dot(xw, w1_ref[...])), w2_ref[...])
    lw_ref[...] = -jnp.exp(-_softplus(-w_pre) - 0.5)
    a = _sigmoid(vecs_ref[1:2] + _dot(_dot(xa, a1_ref[...]), a2_ref[...]))
    if has_vfirst:
        nu = _sigmoid(v0_ref[...] + _dot(_dot(xv, v1_ref[...]), v2_ref[...]))
        v = v + (vfirst_ref[...] - v) * nu
    g_ref[...] = _dot(_sigmoid(_dot(xg, g1_ref[...])), g2_ref[...])
    r_ref[...] = r
    v_ref[...] = v
    a_ref[...] = a
    kk_ref[...] = k * vecs_ref[2:3]
    k_ref[...] = k * (1.0 + (a - 1.0) * vecs_ref[3:4])


def _rwkv_in(h, shift_init, v_first, gain, mix, w_in, w1, w2, a1, a2, g1, g2, v1, v2, v0, vecs, *, tm, shift,
             blocks_per_seq):
    n, d = h.shape
    sw = vecs.shape[1]
    n_seq = shift_init.shape[0]
    has_vfirst = v_first is not None
    row = lambda i: (i, 0)
    seq = lambda i: (i // blocks_per_seq, 0, 0)
    lora = lambda wa, wb: (_pad_cols(wa, LORA_PAD).astype(BF16), _pad_rows(wb, LORA_PAD).astype(BF16))
    w1p, w2p = lora(w1, w2)
    a1p, a2p = lora(a1, a2)
    g1p, g2p = lora(g1, g2)
    ins = [h, shift_init]
    specs = [pl.BlockSpec((tm, d), row), pl.BlockSpec((1, shift, d), seq)]
    if has_vfirst:
        ins.append(v_first)
        specs.append(pl.BlockSpec((tm, sw), row))
    consts = [gain.reshape(1, d), _pad_rows(mix, 8), w_in.astype(BF16), w1p, w2p, a1p, a2p, g1p, g2p]
    if has_vfirst:
        v1p, v2p = lora(v1, v2)
        consts += [v1p, v2p, v0.reshape(1, sw)]
    consts.append(_pad_rows(vecs, 8))
    ins += consts
    specs += [_const_spec(c.shape) for c in consts]
    wide = jax.ShapeDtypeStruct((n, sw), F32)
    outs = [wide] * 7 + [jax.ShapeDtypeStruct((n, MEM_WIDTH), BF16), jax.ShapeDtypeStruct((n_seq, shift, d), F32)]
    out_specs = [pl.BlockSpec((tm, sw), row)] * 7 + [pl.BlockSpec((tm, MEM_WIDTH), row),
                                                     pl.BlockSpec((1, shift, d), seq)]
    return pl.pallas_call(
        functools.partial(_rwkv_in_kernel, sw=sw, shift=shift, blocks_per_seq=blocks_per_seq, has_vfirst=has_vfirst),
        grid=(n // tm,),
        in_specs=specs, out_specs=out_specs, out_shape=outs,
        scratch_shapes=[pltpu.VMEM((shift, d), F32)],
        compiler_params=_params(1),
        name="rwkv_in",
    )(*ins)


def _wkv_kernel(*refs, chunk, n_chunks, heads, has_init):
    if has_init:
        (r_ref, lw_ref, k_ref, v_ref, kk_ref, a_ref, g_ref, vecs_ref, s0_ref, o_ref, sfin_ref, s_scr) = refs
    else:
        (r_ref, lw_ref, k_ref, v_ref, kk_ref, a_ref, g_ref, vecs_ref, o_ref, sfin_ref, s_scr) = refs
    tb = pl.program_id(2)
    c, nc, hd = chunk, n_chunks, HEAD_DIM
    n = c * nc

    @pl.when(tb == 0)
    def _():
        if has_init:
            s_scr[...] = s0_ref[0]
        else:
            s_scr[...] = jnp.zeros_like(s_scr)

    ri = lax.broadcasted_iota(jnp.int32, (n, n), 0)
    ci = lax.broadcasted_iota(jnp.int32, (n, n), 1)
    same = (ri // c) == (ci // c)
    incl = same & (ci <= ri)
    strict = same & (ci < ri)
    eye = (ci == ri).astype(F32)
    own = ((lax.broadcasted_iota(jnp.int32, (n, nc * hd), 0) // c)
           == (lax.broadcasted_iota(jnp.int32, (n, nc * hd), 1) // hd))
    n_double = max(c.bit_length() - 2, 0)

    def expand(x):
        return jnp.where(own, jnp.concatenate([x] * nc, axis=1), 0.0).astype(BF16)

    lw = lw_ref[...]
    cs = _cumsum_chunks(lw, c)
    g_inc_all = jnp.exp(cs)
    g_exc_all = jnp.exp(cs - lw)
    g_inv_all = jnp.exp(-cs)
    r_all, k_all, v_all = r_ref[...], k_ref[...], v_ref[...]
    kk_all, a_all, gate_all = kk_ref[...], a_ref[...], g_ref[...]

    hs = range(heads)
    sls = [slice(h * hd, (h + 1) * hd) for h in hs]
    r_ = [r_all[:, sl] for sl in sls]
    k_ = [k_all[:, sl] for sl in sls]
    v_ = [v_all[:, sl] for sl in sls]
    g_inc = [g_inc_all[:, sl] for sl in sls]
    kk_ = [kk_all[:, sl] for sl in sls]
    kk_ = [x / jnp.maximum(jnp.sqrt(jnp.sum(x * x, axis=-1, keepdims=True)), 1e-12) for x in kk_]
    rt = [r_[h] * g_inc[h] for h in hs]
    at = [-kk_[h] * g_exc_all[:, sls[h]] for h in hs]
    bt = [kk_[h] * a_all[:, sls[h]] * g_inv_all[:, sls[h]] for h in hs]
    kt = [k_[h] * g_inv_all[:, sls[h]] for h in hs]
    bk = [jnp.concatenate([bt[h], kt[h]], axis=0).astype(BF16) for h in hs]
    gram = [_dot_nt(jnp.concatenate([rt[h], at[h]], axis=0), bk[h]) for h in hs]
    m_rb = [jnp.where(incl, x[:n, :n], 0.0).astype(BF16) for x in gram]
    m_rk = [jnp.where(incl, x[:n, n:], 0.0).astype(BF16) for x in gram]
    m_ab = [jnp.where(strict, x[n:, :n], 0.0) for x in gram]
    m_ak = [jnp.where(strict, x[n:, n:], 0.0) for x in gram]
    inv = [eye + x for x in m_ab]
    p = m_ab
    for _ in range(n_double):
        p = [_dot(x, x) for x in p]
        inv = [inv[h] + _dot(inv[h], p[h]) for h in hs]
    inv = [x.astype(BF16) for x in inv]
    vb = [x.astype(BF16) for x in v_]
    w = [_dot(m_ak[h], vb[h]) for h in hs]
    a1 = [_dot(inv[h], at[h]) for h in hs]
    u = [_dot(inv[h], w[h]) for h in hs]
    r1 = [rt[h] + _dot(m_rb[h], a1[h]) for h in hs]
    y0 = [_dot(jnp.concatenate([m_rb[h], m_rk[h]], axis=1), jnp.concatenate([u[h].astype(BF16), vb[h]], axis=0))
          for h in hs]
    phi = [_dot_tn(expand(a1[h]), bt[h]) for h in hs]
    psi = [_dot_tn(jnp.concatenate([expand(u[h]), expand(v_[h])], axis=0), bk[h]) for h in hs]
    s = [s_scr[h] for h in hs]
    ys = [[] for _ in hs]
    for j in range(nc):
        rows = slice(j * c, (j + 1) * c)
        blk = slice(j * hd, (j + 1) * hd)
        for h in hs:
            ys[h].append(_dot_nt(r1[h][rows], s[h]) + y0[h][rows])
        s = [(s[h] + _dot(s[h], phi[h][blk]) + psi[h][blk]) * g_inc[h][(j + 1) * c - 1:(j + 1) * c] for h in hs]
    outs = []
    for h in hs:
        s_scr[h] = s[h]
        sl = sls[h]
        y = jnp.concatenate(ys[h], axis=0) if nc > 1 else ys[h][0]
        mu = jnp.mean(y, axis=-1, keepdims=True)
        var = jnp.mean(jnp.square(y - mu), axis=-1, keepdims=True)
        yn = (y - mu) * lax.rsqrt(var + GN_EPS) * vecs_ref[5:6, sl] + vecs_ref[6:7, sl]
        bonus = jnp.sum(r_[h] * k_[h] * vecs_ref[4:5, sl], axis=-1, keepdims=True) * v_[h]
        outs.append((yn + bonus) * gate_all[:, sl])
    o_ref[...] = jnp.concatenate(outs, axis=-1)

    @pl.when(tb == pl.num_programs(2) - 1)
    def _():
        sfin_ref[0] = s_scr[...]


def _wkv(r, lw, k, v, kk, a, g, vecs, s0, *, n_seq, seq_len, chunk, n_chunks):
    n, sw = r.shape
    n_heads = sw // HEAD_DIM
    width = WKV_HEADS * HEAD_DIM
    tb = chunk * n_chunks
    nb = seq_len // tb
    blk = pl.BlockSpec((tb, width), lambda b, hp, t: (b * nb + t, hp))
    ins = [r, lw, k, v, kk, a, g, _pad_rows(vecs, 8)]
    specs = [blk] * 7 + [pl.BlockSpec((8, width), lambda b, hp, t: (0, hp))]
    state_spec = pl.BlockSpec((1, WKV_HEADS, HEAD_DIM, HEAD_DIM), lambda b, hp, t: (b, hp, 0, 0))
    if s0 is not None:
        ins.append(s0)
        specs.append(state_spec)
    return pl.pallas_call(
        functools.partial(_wkv_kernel, chunk=chunk, n_chunks=n_chunks, heads=WKV_HEADS, has_init=s0 is not None),
        grid=(n_seq, n_heads // WKV_HEADS, nb),
        in_specs=specs,
        out_specs=[blk, state_spec],
        out_shape=[jax.ShapeDtypeStruct((n, sw), F32),
                   jax.ShapeDtypeStruct((n_seq, n_heads, HEAD_DIM, HEAD_DIM), F32)],
        scratch_shapes=[pltpu.VMEM((WKV_HEADS, HEAD_DIM, HEAD_DIM), F32)],
        compiler_params=_params(3),
        name="wkv",
    )(*ins)


def _mem_attn_kernel(q_ref, k_ref, v_ref, o_ref):
    q = q_ref[0]
    k = k_ref[0].astype(BF16)
    v = v_ref[0].astype(BF16)
    outs = []
    for h in range(H_MEM):
        sl = slice(h * HEAD_DIM, (h + 1) * HEAD_DIM)
        s = _dot_nt(q[:, sl], k[:, sl]) * ATTN_SCALE
        p = jnp.exp(s - jnp.max(s, axis=-1, keepdims=True))
        p = p / jnp.sum(p, axis=-1, keepdims=True)
        outs.append(_dot(p, v[:, sl]))
    o_ref[0] = jnp.concatenate(outs, axis=-1).astype(BF16)


def _mem_attn(cq, mem_k, mem_v, *, tq):
    n_seq, t, _ = cq.shape
    qspec = pl.BlockSpec((1, tq, MEM_WIDTH), lambda b, i: (b, i, 0))
    mspec = pl.BlockSpec((1, N_MEM, MEM_WIDTH), lambda b, i: (b, 0, 0))
    return pl.pallas_call(
        _mem_attn_kernel,
        grid=(n_seq, t // tq),
        in_specs=[qspec, mspec, mspec],
        out_specs=qspec,
        out_shape=jax.ShapeDtypeStruct(cq.shape, BF16),
        compiler_params=_params(2),
        name="mem_attn",
    )(cq, mem_k, mem_v)


def _out_proj_kernel(os_ref, om_ref, h_ref, w_ref, gain_ref, o_ref, *, sw):
    mixed = _dot(os_ref[...], w_ref[0:sw, :]) + _dot(om_ref[...], w_ref[sw:, :])
    o_ref[...] = h_ref[...] + _rms(mixed, gain_ref[...])


def _out_proj(o_self, o_mem, h, w_out, gain, *, tm):
    n, d = h.shape
    sw = o_self.shape[1]
    row = lambda i: (i, 0)
    return pl.pallas_call(
        functools.partial(_out_proj_kernel, sw=sw),
        grid=(n // tm,),
        in_specs=[pl.BlockSpec((tm, sw), row), pl.BlockSpec((tm, MEM_WIDTH), row), pl.BlockSpec((tm, d), row),
                  _const_spec(w_out.shape), _const_spec((1, d))],
        out_specs=pl.BlockSpec((tm, d), row),
        out_shape=jax.ShapeDtypeStruct((n, d), F32),
        compiler_params=_params(1),
        name="out_proj",
    )(o_self, o_mem, h, w_out.astype(BF16), gain.reshape(1, d))


def _gelu_tanh(x):
    return 0.5 * x * (1.0 + jnp.tanh(0.7978845608028654 * (x + 0.044715 * (x * x * x))))


def _ffn_kernel(h_ref, init_ref, gin_ref, gout_ref, wup_ref, cw_ref, cb_ref, wdn_ref, o_ref, conv_ref, carry_ref,
                acc_ref, *, d_ff, shift, blocks_per_seq, ff_chunk):
    @pl.when(pl.program_id(0) % blocks_per_seq == 0)
    def _():
        carry_ref[...] = init_ref[0]

    h = h_ref[...]
    tm = h.shape[0]
    xn = _rms(h, gin_ref[...]).astype(BF16)
    for j in range(d_ff // ff_chunk):
        cols = slice(j * ff_chunk, (j + 1) * ff_chunk)
        u = _dot(xn, wup_ref[:, cols])
        g = _dot(xn, wup_ref[:, d_ff + j * ff_chunk:d_ff + (j + 1) * ff_chunk])
        carry = carry_ref[:, cols]
        g1 = _shift_rows(g, carry, shift)
        g2 = _shift_rows(g1, carry[:shift], shift)
        gc = cb_ref[:, cols] + g2 * cw_ref[0:1, cols] + g1 * cw_ref[1:2, cols] + g * cw_ref[2:3, cols]
        carry_ref[:, cols] = g[tm - 2 * shift:]
        conv_ref[0, :, cols] = g[tm - 2 * shift:]
        part = _dot(_gelu_tanh(gc) * u, wdn_ref[cols, :])
        if j == 0:
            acc_ref[...] = part
        else:
            acc_ref[...] += part
    o_ref[...] = h + _rms(acc_ref[...], gout_ref[...])


def _ffn(h, conv_init, gain_in, gain_out, w_up, conv_w, conv_b, w_down, *, tm, shift, blocks_per_seq):
    n, d = h.shape
    d_ff = w_down.shape[0]
    n_seq = conv_init.shape[0]
    row = lambda i: (i, 0)
    seq = lambda i: (i // blocks_per_seq, 0, 0)
    once = pl.Buffered(1)
    return pl.pallas_call(
        functools.partial(_ffn_kernel, d_ff=d_ff, shift=shift, blocks_per_seq=blocks_per_seq, ff_chunk=2 * LANES),
        grid=(n // tm,),
        in_specs=[pl.BlockSpec((tm, d), row), pl.BlockSpec((1, 2 * shift, d_ff), seq),
                  _const_spec((1, d)), _const_spec((1, d)),
                  pl.BlockSpec(w_up.shape, lambda i: (0, 0), pipeline_mode=once),
                  _const_spec((8, d_ff)), _const_spec((1, d_ff)),
                  pl.BlockSpec(w_down.shape, lambda i: (0, 0), pipeline_mode=once)],
        out_specs=[pl.BlockSpec((tm, d), row), pl.BlockSpec((1, 2 * shift, d_ff), seq)],
        out_shape=[jax.ShapeDtypeStruct((n, d), F32), jax.ShapeDtypeStruct((n_seq, 2 * shift, d_ff), F32)],
        scratch_shapes=[pltpu.VMEM((2 * shift, d_ff), F32), pltpu.VMEM((tm, d), F32)],
        compiler_params=_params(1),
        name="ffn",
    )(h, conv_init, gain_in.reshape(1, d), gain_out.reshape(1, d), w_up.astype(BF16), _pad_rows(conv_w, 8),
      conv_b.reshape(1, d_ff), w_down.astype(BF16))


def _head_mean_sq(x, ones_bd):
    outs = []
    for j in range(x.shape[1] // 256):
        sq = jnp.square(x[:, j * 256:(j + 1) * 256])
        hi = sq.astype(BF16)
        lo = (sq - hi.astype(F32)).astype(BF16)
        outs.append(jnp.dot(hi, ones_bd, preferred_element_type=F32) + jnp.dot(lo, ones_bd, preferred_element_type=F32))
    return jnp.concatenate(outs, axis=-1) * (1.0 / HEAD_DIM)


def _feat_base(h):
    return FEAT * (h // 2) + HEAD_DIM * (1 - h % 2)


def _bias_features(c, n_heads):
    terms = jnp.concatenate(_split3(c * LOG2E), axis=1)
    r = lax.broadcasted_iota(jnp.int32, (3 * LANES, LANES), 0)
    l = lax.broadcasted_iota(jnp.int32, (3 * LANES, LANES), 1)
    head = r % LANES
    off = l - _feat_base(head)
    in_head = (off >= 0) & (off < FEAT) & (head < n_heads)
    place_q = (in_head & (off == r // LANES)).astype(BF16)
    place_k = (in_head & (off == 3 + r // LANES)).astype(BF16)
    lane = lax.broadcasted_iota(jnp.int32, (1, LANES), 1)
    slot = lane % FEAT
    used = ((lane % HEAD_DIM) // FEAT) < (n_heads // 2)
    ones_q = (used & (slot >= 3) & (slot < 6)).astype(F32)
    ones_k = (used & (slot < 3)).astype(F32)
    qf = jnp.dot(terms, place_q, preferred_element_type=F32) + ones_q
    kf = ones_k - jnp.dot(terms, place_k, preferred_element_type=F32)
    return qf.astype(BF16), kf.astype(BF16)


def _fox_in_kernel(h_ref, gain_ref, w_ref, bf_ref, qkg_ref, q_ref, k_ref, v_ref, kb_ref, vb_ref, lf_ref, c_ref,
                   qf_ref, kf_ref, og_ref, cq_ref, carry_ref, *, sw, n_heads, shift, steps, blocks_per_seq):
    xn = _rms(h_ref[...], gain_ref[...]).astype(BF16)
    tm = xn.shape[0]
    ri = lax.broadcasted_iota(jnp.int32, (256, 256), 0) // HEAD_DIM
    ci = lax.broadcasted_iota(jnp.int32, (256, 256), 1) // HEAD_DIM
    ones_bd = (ri == ci).astype(BF16)
    q = jnp.dot(xn, w_ref[:, 0:sw], preferred_element_type=F32)
    q = q * lax.rsqrt(_head_mean_sq(q, ones_bd) + NORM_EPS) * qkg_ref[0:1]
    q_ref[...] = (q * (ATTN_SCALE * LOG2E)).astype(BF16)
    k = jnp.dot(xn, w_ref[:, sw:2 * sw], preferred_element_type=F32)
    k = k * lax.rsqrt(_head_mean_sq(k, ones_bd) + NORM_EPS) * qkg_ref[1:2]
    k_ref[...] = k
    kb_ref[...] = k.astype(BF16)
    v = jnp.dot(xn, w_ref[:, 2 * sw:3 * sw], preferred_element_type=F32)
    v_ref[...] = v
    vb_ref[...] = v.astype(BF16)
    o0 = 3 * sw + LANES
    lf = -_softplus(-(jnp.dot(xn, w_ref[:, 3 * sw:o0], preferred_element_type=F32) + bf_ref[...]))
    lf_ref[...] = lf
    og_ref[...] = _sigmoid(jnp.dot(xn, w_ref[:, o0:o0 + sw], preferred_element_type=F32))
    cq_ref[...] = jnp.dot(xn, w_ref[:, o0 + sw:], preferred_element_type=F32).astype(BF16)
    if steps is None:
        @pl.when(pl.program_id(0) % blocks_per_seq == 0)
        def _():
            carry_ref[...] = jnp.zeros_like(carry_ref)

        c = _cumsum_chunks(lf, tm) + carry_ref[0:1]
        carry_ref[...] = jnp.broadcast_to(c[tm - 1:tm], carry_ref.shape)
    else:
        parts = [lf[0:shift]]
        for t in range(1, steps):
            parts.append(parts[-1] + lf[t * shift:(t + 1) * shift])
        c = jnp.concatenate(parts, axis=0)
    c_ref[...] = c
    qf, kf = _bias_features(c, n_heads)
    qf_ref[...] = qf
    kf_ref[...] = kf


def _fox_in(h, gain, w_in, bf, qk_gain, *, tm, shift, steps, blocks_per_seq):
    n, d = h.shape
    n_heads = bf.shape[0]
    sw = n_heads * HEAD_DIM
    w = jnp.concatenate([w_in[:, :3 * sw], _pad_cols(w_in[:, 3 * sw:3 * sw + n_heads], LANES),
                         w_in[:, 3 * sw + n_heads:]], axis=1).astype(BF16)
    qkg = _pad_rows(jnp.tile(qk_gain, (1, n_heads)), 8)
    row = lambda i: (i, 0)
    wide = pl.BlockSpec((tm, sw), row)
    lane = pl.BlockSpec((tm, LANES), row)
    wide_f32 = jax.ShapeDtypeStruct((n, sw), F32)
    wide_bf16 = jax.ShapeDtypeStruct((n, sw), BF16)
    lane_f32 = jax.ShapeDtypeStruct((n, LANES), F32)
    lane_bf16 = jax.ShapeDtypeStruct((n, LANES), BF16)
    return pl.pallas_call(
        functools.partial(_fox_in_kernel, sw=sw, n_heads=n_heads, shift=shift, steps=steps,
                          blocks_per_seq=blocks_per_seq),
        grid=(n // tm,),
        in_specs=[pl.BlockSpec((tm, d), row), _const_spec((1, d)), _const_spec(w.shape), _const_spec((1, LANES)),
                  _const_spec((8, sw))],
        out_specs=[wide, wide, wide, wide, wide, lane, lane, lane, lane, wide, pl.BlockSpec((tm, MEM_WIDTH), row)],
        out_shape=[wide_bf16, wide_f32, wide_f32, wide_bf16, wide_bf16, lane_f32, lane_f32, lane_bf16, lane_bf16,
                   wide_f32, jax.ShapeDtypeStruct((n, MEM_WIDTH), BF16)],
        scratch_shapes=[pltpu.VMEM((8, LANES), F32)],
        compiler_params=_params(1),
        name="fox_in",
    )(h, gain.reshape(1, d), w, _pad_cols(bf.reshape(1, n_heads), LANES), qkg)


def _fox_flash_kernel(q_ref, k_ref, v_ref, qf_ref, kf_ref, og_ref, o_ref, m_ref, l_ref, acc_ref, *, n_heads, tq, tk):
    qi = pl.program_id(1)
    ki = pl.program_id(2)
    last = ((qi + 1) * tq - 1) // tk

    @pl.when(ki == 0)
    def _():
        m_ref[...] = jnp.full_like(m_ref, NEG_BIG)
        l_ref[...] = jnp.zeros_like(l_ref)
        acc_ref[...] = jnp.zeros_like(acc_ref)

    lane = lax.broadcasted_iota(jnp.int32, (1, LANES), 1)
    upper = lane >= HEAD_DIM
    zero = jnp.zeros((), BF16)

    def step(masked):
        if masked:
            qpos = qi * tq + lax.broadcasted_iota(jnp.int32, (tq, tk), 0)
            kpos = ki * tk + lax.broadcasted_iota(jnp.int32, (tq, tk), 1)
            causal = kpos <= qpos
        qf = qf_ref[...]
        kf = kf_ref[...]
        for pr in range(n_heads // 2):
            cols = slice(pr * LANES, (pr + 1) * LANES)
            heads = (2 * pr, 2 * pr + 1)
            q2, k2, v2 = q_ref[:, cols], k_ref[:, cols], v_ref[:, cols]
            own = [upper == bool(h % 2) for h in heads]
            feat = [(lane >= _feat_base(h)) & (lane < _feat_base(h) + FEAT) for h in heads]
            q_aug = [jnp.where(own[i], q2, jnp.where(feat[i], qf, zero)) for i in range(2)]
            k_aug = [jnp.where(own[i], k2, kf) for i in range(2)]
            s = [_dot_nt(q_aug[i], k_aug[i]) for i in range(2)]
            if masked:
                s = [jnp.where(causal, x, NEG_BIG) for x in s]
            m_old = [m_ref[:, h:h + 1] for h in heads]
            m_new = [jnp.maximum(m_old[i], jnp.max(s[i], axis=-1, keepdims=True)) for i in range(2)]
            alpha = [jnp.exp2(m_old[i] - m_new[i]) for i in range(2)]
            p = [jnp.exp2(s[i] - m_new[i]) for i in range(2)]
            for i, h in enumerate(heads):
                l_ref[:, h:h + 1] = alpha[i] * l_ref[:, h:h + 1] + jnp.sum(p[i], axis=-1, keepdims=True)
                m_ref[:, h:h + 1] = m_new[i]
            pv = [jnp.dot(p[i].astype(BF16), v2, preferred_element_type=F32) for i in range(2)]
            acc = acc_ref[:, cols]
            acc_ref[:, cols] = jnp.where(upper, alpha[1] * acc + pv[1], alpha[0] * acc + pv[0])

    @pl.when(ki < last)
    def _():
        step(False)

    @pl.when(ki == last)
    def _():
        step(True)
        for pr in range(n_heads // 2):
            cols = slice(pr * LANES, (pr + 1) * LANES)
            inv_l = jnp.where(upper, 1.0 / l_ref[:, 2 * pr + 1:2 * pr + 2], 1.0 / l_ref[:, 2 * pr:2 * pr + 1])
            o_ref[:, cols] = acc_ref[:, cols] * inv_l * og_ref[:, cols]


def _fox_flash(q, k, v, qf, kf, og, *, n_seq, seq_len, tq, tk):
    n, sw = k.shape
    n_heads = sw // HEAD_DIM
    nq, nk = seq_len // tq, seq_len // tk
    qrow = lambda b, i, j: (b * nq + i, 0)
    krow = lambda b, i, j: (b * nk + jnp.minimum(j, ((i + 1) * tq - 1) // tk), 0)
    return pl.pallas_call(
        functools.partial(_fox_flash_kernel, n_heads=n_heads, tq=tq, tk=tk),
        grid=(n_seq, nq, nk),
        in_specs=[pl.BlockSpec((tq, sw), qrow), pl.BlockSpec((tk, sw), krow), pl.BlockSpec((tk, sw), krow),
                  pl.BlockSpec((tq, LANES), qrow), pl.BlockSpec((tk, LANES), krow), pl.BlockSpec((tq, sw), qrow)],
        out_specs=pl.BlockSpec((tq, sw), qrow),
        out_shape=jax.ShapeDtypeStruct((n, sw), F32),
        scratch_shapes=[pltpu.VMEM((tq, LANES), F32), pltpu.VMEM((tq, LANES), F32), pltpu.VMEM((tq, sw), F32)],
        compiler_params=_params(3),
        name="fox_flash",
    )(q, k, v, qf, kf, og)


HEAD_ROWS = 16
PAGE_GROUP = 8


def _fox_paged_kernel(pt_ref, q_ref, cn_ref, kn_ref, vn_ref, og_ref, k_hbm, v_hbm, lp_hbm, o_ref,
                      kbuf, vbuf, lpbuf, sem, m_ref, l_ref, acc_ref, suf_ref, *, layer, n_heads, n_pages, group, page):
    i = pl.program_id(0)
    j = pl.program_id(1)
    n_groups = pl.num_programs(1)
    g = i * n_groups + j
    hd = HEAD_DIM
    hs = range(n_heads)

    def copies(seq, grp, slot):
        out = []
        for u in range(group):
            pg = pt_ref[seq, n_pages - 1 - (grp * group + u)]
            out.append(pltpu.make_async_copy(k_hbm.at[layer, pg], kbuf.at[slot, u], sem.at[slot, 0]))
            out.append(pltpu.make_async_copy(v_hbm.at[layer, pg], vbuf.at[slot, u], sem.at[slot, 1]))
            out.append(pltpu.make_async_copy(lp_hbm.at[layer, pg], lpbuf.at[slot, u, pl.ds(0, n_heads)], sem.at[slot, 2]))
        return out

    @pl.when(g == 0)
    def _():
        lpbuf[:, :, n_heads:, :] = jnp.zeros((2, group, HEAD_ROWS - n_heads, page), F32)
        for cp in copies(0, 0, 0):
            cp.start()

    @pl.when(g + 1 < pl.num_programs(0) * n_groups)
    def _():
        wrap = j + 1 == n_groups
        for cp in copies(jnp.where(wrap, i + 1, i), jnp.where(wrap, 0, j + 1), (g + 1) % 2):
            cp.start()

    slot = g % 2
    q = q_ref[0]
    q_ = [q[:, h * hd:(h + 1) * hd] for h in hs]

    @pl.when(j == 0)
    def _():
        kn, vn = kn_ref[0], vn_ref[0]
        steps = q.shape[0]
        trow = lax.broadcasted_iota(jnp.int32, (steps, steps), 0)
        tcol = lax.broadcasted_iota(jnp.int32, (steps, steps), 1)
        s = [jnp.where(tcol <= trow,
                       _dot_nt(q_[h], kn[:, h * hd:(h + 1) * hd]) + LOG2E * cn_ref[0, h][:, 0:steps], NEG_BIG)
             for h in hs]
        m = [jnp.max(x, axis=-1, keepdims=True) for x in s]
        p = [jnp.exp2(s[h] - m[h]) for h in hs]
        for h in hs:
            m_ref[h] = m[h]
            l_ref[h] = jnp.sum(p[h], axis=-1, keepdims=True)
            acc_ref[h] = _dot(p[h], vn[:, h * hd:(h + 1) * hd])
        suf_ref[...] = jnp.zeros_like(suf_ref)

    for cp in copies(i, j, slot):
        cp.wait()

    lp = [lpbuf[slot, u] for u in range(group)]
    ri = lax.broadcasted_iota(jnp.int32, (page, page), 0)
    ci = lax.broadcasted_iota(jnp.int32, (page, page), 1)
    later = (ri > ci).astype(BF16)
    hi, mid, lo = _split3(jnp.concatenate(lp, axis=0))
    within = (jnp.dot(hi, later, preferred_element_type=F32) + jnp.dot(mid, later, preferred_element_type=F32)
              + jnp.dot(lo, later, preferred_element_type=F32))
    carry = [suf_ref[...]]
    for u in range(group):
        carry.append(carry[-1] + jnp.sum(lp[u], axis=-1, keepdims=True))
    suf_ref[...] = carry[group]
    n_self = q.shape[0]
    bias = [jnp.concatenate([within[u * HEAD_ROWS + h:u * HEAD_ROWS + h + 1] + carry[u][h:h + 1]
                             for u in range(group)], axis=1) for h in hs]
    kcat = [jnp.concatenate([kbuf[slot, u, h] for u in range(group)], axis=1).astype(BF16) for h in hs]
    vcat = [jnp.concatenate([vbuf[slot, u, h] for u in range(group)], axis=1).astype(BF16) for h in hs]
    s = [_dot(q_[h], kcat[h]) + LOG2E * (bias[h] + cn_ref[0, h][:, n_self:n_self + 1]) for h in hs]
    m_old = [m_ref[h] for h in hs]
    m_new = [jnp.maximum(m_old[h], jnp.max(s[h], axis=-1, keepdims=True)) for h in hs]
    alpha = [jnp.exp2(m_old[h] - m_new[h]) for h in hs]
    p = [jnp.exp2(s[h] - m_new[h]) for h in hs]
    pv = [_dot_nt(p[h], vcat[h]) for h in hs]
    for h in hs:
        l_ref[h] = alpha[h] * l_ref[h] + jnp.sum(p[h], axis=-1, keepdims=True)
        acc_ref[h] = alpha[h] * acc_ref[h] + pv[h]
        m_ref[h] = m_new[h]

    @pl.when(j == n_groups - 1)
    def _():
        o_ref[0] = jnp.concatenate([acc_ref[h] / l_ref[h] for h in hs], axis=-1) * og_ref[0]


def _fox_paged(page_table, q, cn, k_new, v_new, og, cache_k, cache_v, cache_lp_t, *, layer, group):
    b, steps, sw = q.shape
    n_heads = sw // HEAD_DIM
    n_pages = page_table.shape[1]
    page = cache_k.shape[4]
    seq3 = lambda i, j, pt: (i, 0, 0)
    seq4 = lambda i, j, pt: (i, 0, 0, 0)
    tok = pl.BlockSpec((1, steps, sw), seq3)
    hbm = pl.BlockSpec(memory_space=pl.ANY)
    grid_spec = pltpu.PrefetchScalarGridSpec(
        num_scalar_prefetch=1,
        grid=(b, n_pages // group),
        in_specs=[tok, pl.BlockSpec((1, n_heads, steps, 2 * steps), seq4), tok, tok, tok, hbm, hbm, hbm],
        out_specs=tok,
        scratch_shapes=[pltpu.VMEM((2, group, n_heads, HEAD_DIM, page), F32),
                        pltpu.VMEM((2, group, n_heads, HEAD_DIM, page), F32),
                        pltpu.VMEM((2, group, HEAD_ROWS, page), F32),
                        pltpu.SemaphoreType.DMA((2, 3)),
                        pltpu.VMEM((n_heads, steps, 1), F32), pltpu.VMEM((n_heads, steps, 1), F32),
                        pltpu.VMEM((n_heads, steps, HEAD_DIM), F32), pltpu.VMEM((HEAD_ROWS, 1), F32)],
    )
    return pl.pallas_call(
        functools.partial(_fox_paged_kernel, layer=layer, n_heads=n_heads, n_pages=n_pages, group=group, page=page),
        grid_spec=grid_spec,
        out_shape=jax.ShapeDtypeStruct((b, steps, sw), F32),
        compiler_params=_params(2),
        name="fox_paged",
    )(page_table, q, cn, k_new, v_new, og, cache_k, cache_v, cache_lp_t)


def _to_time_major(x):
    x = jnp.swapaxes(x, 0, 1)
    return x.reshape((x.shape[0] * x.shape[1],) + x.shape[2:])


def _to_batch_major(x, b):
    return jnp.swapaxes(x.reshape((x.shape[0] // b, b) + x.shape[1:]), 0, 1)


def _trunk(h, P, *, prompt, n_seq, seq_len, mem_k, mem_v, rw_shift, rw_wkv, ffn_conv, fox_cache):
    depth = P["norm_gains"].shape[0]
    n, d = h.shape
    sw = P["rw_vecs"].shape[2]
    n_heads = sw // HEAD_DIM
    if prompt:
        tm, shift, blocks_per_seq = 256, 1, seq_len // 256
    else:
        tm, shift, blocks_per_seq = n, n_seq, 1
    v_first = None
    shifts, wkvs, fks, fvs, fls, convs = [], [], [], [], [], []
    for l in range(depth):
        gains = P["norm_gains"][l]
        i = l // 2
        if l % 2 == 0:
            init = jnp.zeros((n_seq, 1, d), F32) if prompt else rw_shift[i][None]
            vj = None if i == 0 else i - 1
            r, lw, k, v, kk, a, g, cq, sh = _rwkv_in(
                h, init, v_first, gains[0], P["rw_mix"][i], P["rw_w_in"][i], P["rw_w1"][i], P["rw_w2"][i],
                P["rw_a1"][i], P["rw_a2"][i], P["rw_g1"][i], P["rw_g2"][i],
                None if vj is None else P["rw_v1"][vj], None if vj is None else P["rw_v2"][vj],
                None if vj is None else P["rw_v0"][vj], P["rw_vecs"][i],
                tm=tm, shift=shift, blocks_per_seq=blocks_per_seq)
            if v_first is None:
                v_first = v
            if prompt:
                o_self, s_fin = _wkv(r, lw, k, v, kk, a, g, P["rw_vecs"][i], None, n_seq=n_seq, seq_len=seq_len,
                                     chunk=WKV_CHUNK, n_chunks=4)
                shifts.append(sh[:, 0])
            else:
                pad = lambda x: jnp.pad(_to_batch_major(x, n_seq), ((0, 0), (0, 8 - seq_len), (0, 0))).reshape(n_seq * 8, sw)
                o_pad, s_fin = _wkv(*[pad(x) for x in (r, lw, k, v, kk, a, g)], P["rw_vecs"][i], rw_wkv[i],
                                    n_seq=n_seq, seq_len=8, chunk=8, n_chunks=1)
                o_self = _to_time_major(o_pad.reshape(n_seq, 8, sw)[:, :seq_len])
                shifts.append(sh[0])
            wkvs.append(s_fin)
        else:
            q, k, v, kb, vb, lf, c, qf, kf, og, cq = _fox_in(
                h, gains[0], P["fox_w_in"][i], P["fox_bf"][i], P["fox_qk_gain"][i],
                tm=tm, shift=shift, steps=None if prompt else seq_len, blocks_per_seq=blocks_per_seq)
            if prompt:
                o_self = _fox_flash(q, kb, vb, qf, kf, og, n_seq=n_seq, seq_len=seq_len, tq=128,
                                    tk=min(1024, seq_len))
            else:
                cache_k, cache_v, cache_lp_t, page_table = fox_cache
                pad8 = lambda x: jnp.pad(_to_batch_major(x, n_seq), ((0, 0), (0, 8 - seq_len), (0, 0)))
                ch = jnp.swapaxes(pad8(c)[:, :, :n_heads], 1, 2)
                cn = jnp.concatenate([ch[:, :, :, None] - ch[:, :, None, :],
                                      jnp.broadcast_to(ch[..., None], ch.shape + (8,))], axis=-1)
                o_b = _fox_paged(page_table, pad8(q), cn, pad8(k), pad8(v), pad8(og), cache_k, cache_v, cache_lp_t,
                                 layer=i, group=min(PAGE_GROUP, page_table.shape[1]))
                o_self = _to_time_major(o_b[:, :seq_len])
            fks.append(k)
            fvs.append(v)
            fls.append(lf[:, :n_heads])
        if prompt:
            o_mem = _mem_attn(cq.reshape(n_seq, seq_len, MEM_WIDTH), mem_k[l], mem_v[l],
                              tq=min(1024, seq_len)).reshape(n, MEM_WIDTH)
        else:
            cq_b = jnp.pad(_to_batch_major(cq, n_seq), ((0, 0), (0, 16 - seq_len), (0, 0)))
            o_mem = _to_time_major(_mem_attn(cq_b, mem_k[l], mem_v[l], tq=16)[:, :seq_len])
        h = _out_proj(o_self, o_mem, h, P["w_out"][l], gains[1], tm=tm)
        if prompt:
            conv_init = jnp.zeros((n_seq, CONV_W - 1, P["ffn_conv_b"].shape[1]), F32)
        else:
            conv_init = _to_time_major(ffn_conv[l])[None]
        h, cs = _ffn(h, conv_init, gains[2], gains[3], P["ffn_w_up"][l], P["ffn_conv_w"][l], P["ffn_conv_b"][l],
                     P["ffn_w_down"][l], tm=tm, shift=shift, blocks_per_seq=blocks_per_seq)
        convs.append(cs if prompt else _to_batch_major(cs[0], n_seq))
    return h, wkvs, shifts, fks, fvs, fls, convs


def kernel(x_prompt, x_sample, cache_mem_k, cache_mem_v, cache_fox_k, cache_fox_v, cache_fox_logf, state_rwkv_wkv,
           state_rwkv_shift, state_ffn_conv, page_table, mem_prompt, norm_gains, w_out, mem_norm, w_mem_kv, rw_mix,
           rw_w_in, rw_vecs, rw_w1, rw_w2, rw_a1, rw_a2, rw_g1, rw_g2, rw_v0, rw_v1, rw_v2, fox_w_in, fox_bf,
           fox_qk_gain, ffn_w_up, ffn_conv_w, ffn_conv_b, ffn_w_down):
    P = dict(norm_gains=norm_gains, w_out=w_out, rw_mix=rw_mix, rw_w_in=rw_w_in, rw_vecs=rw_vecs, rw_w1=rw_w1,
             rw_w2=rw_w2, rw_a1=rw_a1, rw_a2=rw_a2, rw_g1=rw_g1, rw_g2=rw_g2, rw_v0=rw_v0, rw_v1=rw_v1, rw_v2=rw_v2,
             fox_w_in=fox_w_in, fox_bf=fox_bf, fox_qk_gain=fox_qk_gain, ffn_w_up=ffn_w_up, ffn_conv_w=ffn_conv_w,
             ffn_conv_b=ffn_conv_b, ffn_w_down=ffn_w_down)
    b, s, d = x_prompt.shape
    db, ds, _ = x_sample.shape
    depth = norm_gains.shape[0]
    n_heads = fox_bf.shape[1]
    sw = n_heads * HEAD_DIM

    p_mem_k, p_mem_v = _mem_kv(mem_prompt, mem_norm, w_mem_kv)
    hp, p_wkv, p_shift, p_fk, p_fv, p_fl, p_conv = _trunk(
        x_prompt.reshape(b * s, d), P, prompt=True, n_seq=b, seq_len=s, mem_k=p_mem_k, mem_v=p_mem_v,
        rw_shift=None, rw_wkv=None, ffn_conv=None, fox_cache=None)

    cache_k = jnp.transpose(cache_fox_k, (0, 1, 3, 4, 2))
    cache_v = jnp.transpose(cache_fox_v, (0, 1, 3, 4, 2))
    cache_lp_t = jnp.swapaxes(cache_fox_logf, 2, 3)
    hs, s_wkv, s_shift, s_fk, s_fv, s_fl, s_conv = _trunk(
        _to_time_major(x_sample), P, prompt=False, n_seq=db, seq_len=ds,
        mem_k=cache_mem_k.reshape(depth, db, N_MEM, MEM_WIDTH), mem_v=cache_mem_v.reshape(depth, db, N_MEM, MEM_WIDTH),
        rw_shift=state_rwkv_shift, rw_wkv=state_rwkv_wkv, ffn_conv=state_ffn_conv,
        fox_cache=(cache_k, cache_v, cache_lp_t, page_table))

    heads = lambda x, bb, tt: x.reshape(bb, tt, n_heads, HEAD_DIM)
    mem_heads = lambda x: x.reshape(depth, b, N_MEM, H_MEM, HEAD_DIM)
    return (hp.reshape(b, s, d), _to_batch_major(hs, db),
            jnp.stack(p_wkv), jnp.stack(p_shift),
            jnp.stack([heads(x, b, s) for x in p_fk]), jnp.stack([heads(x, b, s) for x in p_fv]),
            jnp.stack([x.reshape(b, s, n_heads) for x in p_fl]),
            mem_heads(p_mem_k), mem_heads(p_mem_v), jnp.stack(p_conv),
            jnp.stack(s_wkv), jnp.stack(s_shift),
            jnp.stack([heads(_to_batch_major(x, db), db, ds) for x in s_fk]),
            jnp.stack([heads(_to_batch_major(x, db), db, ds) for x in s_fv]),
            jnp.stack([_to_batch_major(x, db) for x in s_fl]),
            jnp.stack(s_conv))
```

```python
import functools

import jax
import jax.numpy as jnp
from jax import lax
from jax.experimental import pallas as pl
from jax.experimental.pallas import tpu as pltpu

F32 = jnp.float32
BF16 = jnp.bfloat16

HEAD_DIM = 64
N_MEM = 256
H_MEM = 4
MEM_WIDTH = H_MEM * HEAD_DIM
NORM_EPS = 1e-6
GN_EPS = 64e-5
ATTN_SCALE = HEAD_DIM ** -0.5
LOG2E = 1.4426950408889634
CONV_W = 3
LANES = 128
LORA_PAD = 128
VMEM_LIMIT = 56 * 1024 * 1024
NEG_BIG = -1e30
WKV_CHUNK = 64
WKV_HEADS = 4
FEAT = 8


def _params(n_axes):
    return pltpu.CompilerParams(dimension_semantics=("arbitrary",) * n_axes, vmem_limit_bytes=VMEM_LIMIT)


def _rms(x, g):
    return x * lax.rsqrt(jnp.mean(x * x, axis=-1, keepdims=True) + NORM_EPS) * g


def _dot(a, b):
    return jnp.dot(a.astype(BF16), b.astype(BF16), preferred_element_type=F32)


def _dot_nt(a, b):
    return lax.dot_general(a.astype(BF16), b.astype(BF16), (((1,), (1,)), ((), ())), preferred_element_type=F32)


def _dot_tn(a, b):
    return lax.dot_general(a.astype(BF16), b.astype(BF16), (((0,), (0,)), ((), ())), preferred_element_type=F32)


def _sigmoid(x):
    return 1.0 / (1.0 + jnp.exp(-x))


def _softplus(x):
    return jnp.maximum(x, 0.0) + jnp.log(1.0 + jnp.exp(-jnp.abs(x)))


def _split3(x):
    hi = x.astype(BF16)
    r1 = x - hi.astype(F32)
    mid = r1.astype(BF16)
    lo = (r1 - mid.astype(F32)).astype(BF16)
    return hi, mid, lo


def _shift_rows(x, carry, shift):
    n = x.shape[0]
    c = carry.shape[0]
    if shift % 8 == 0:
        return jnp.concatenate([carry[c - shift:], x[:n - shift]], axis=0)
    row = lax.broadcasted_iota(jnp.int32, x.shape, 0)
    out = pltpu.roll(x, shift, axis=0)
    for j in range(shift):
        out = jnp.where(row == j, carry[c - shift + j:c - shift + j + 1], out)
    return out


def _cumsum_chunks(x, c):
    pos = lax.broadcasted_iota(jnp.int32, x.shape, 0) & (c - 1)
    s = 1
    while s < c:
        x = x + jnp.where(pos >= s, pltpu.roll(x, s, axis=0), 0.0)
        s *= 2
    return x


def _const_spec(shape):
    nd = len(shape)
    return pl.BlockSpec(shape, lambda *_: (0,) * nd)


def _pad_cols(w, n):
    return jnp.pad(w, ((0, 0), (0, n - w.shape[1])))


def _pad_rows(w, n):
    return jnp.pad(w, ((0, n - w.shape[0]), (0, 0)))


def _memkv_kernel(mem_ref, gain_ref, w_ref, k_ref, v_ref):
    xn = _rms(mem_ref[0], gain_ref[0])
    kv = _dot(xn, w_ref[0])
    k_ref[0, 0] = kv[:, :MEM_WIDTH]
    v_ref[0, 0] = kv[:, MEM_WIDTH:]


def _mem_kv(mem, mem_norm, w_mem_kv):
    depth, d = mem_norm.shape
    b = mem.shape[0]
    out = jax.ShapeDtypeStruct((depth, b, N_MEM, MEM_WIDTH), F32)
    return pl.pallas_call(
        _memkv_kernel,
        grid=(depth, b),
        in_specs=[pl.BlockSpec((1, N_MEM, d), lambda l, i: (i, 0, 0)),
                  pl.BlockSpec((1, 1, d), lambda l, i: (l, 0, 0)),
                  pl.BlockSpec((1, d, 2 * MEM_WIDTH), lambda l, i: (l, 0, 0))],
        out_specs=[pl.BlockSpec((1, 1, N_MEM, MEM_WIDTH), lambda l, i: (l, i, 0, 0))] * 2,
        out_shape=[out, out],
        compiler_params=_params(2),
        name="mem_kv",
    )(mem, mem_norm.reshape(depth, 1, d), w_mem_kv.astype(BF16))


def _rwkv_in_kernel(*refs, sw, shift, blocks_per_seq, has_vfirst):
    if has_vfirst:
        (h_ref, init_ref, vfirst_ref, gain_ref, mix_ref, w_in_ref, w1_ref, w2_ref, a1_ref, a2_ref, g1_ref, g2_ref,
         v1_ref, v2_ref, v0_ref, vecs_ref,
         r_ref, lw_ref, k_ref, v_ref, kk_ref, a_ref, g_ref, cq_ref, shift_ref, carry_ref) = refs
    else:
        (h_ref, init_ref, gain_ref, mix_ref, w_in_ref, w1_ref, w2_ref, a1_ref, a2_ref, g1_ref, g2_ref, vecs_ref,
         r_ref, lw_ref, k_ref, v_ref, kk_ref, a_ref, g_ref, cq_ref, shift_ref, carry_ref) = refs

    @pl.when(pl.program_id(0) % blocks_per_seq == 0)
    def _():
        carry_ref[...] = init_ref[0]

    xn = _rms(h_ref[...], gain_ref[...])
    tm = xn.shape[0]
    prev = _shift_rows(xn, carry_ref[...], shift)
    carry_ref[...] = xn[tm - shift:]
    shift_ref[0] = xn[tm - shift:]
    xx = prev - xn
    xr, xw, xk, xv, xa, xg = [(xn + xx * mix_ref[j:j + 1]).astype(BF16) for j in range(6)]
    r = _dot(xr, w_in_ref[:, 0:sw])
    k = _dot(xk, w_in_ref[:, sw:2 * sw])
    v = _dot(xv, w_in_ref[:, 2 * sw:3 * sw])
    cq_ref[...] = _dot(xn, w_in_ref[:, 3 * sw:]).astype(BF16)
    w_pre = vecs_ref[0:1] + _dot(jnp.tanh(_dot(xw, w1_ref[...])), w2_ref[...])
    lw_ref[...] = -jnp.exp(-_softplus(-w_pre) - 0.5)
    a = _sigmoid(vecs_ref[1:2] + _dot(_dot(xa, a1_ref[...]), a2_ref[...]))
    if has_vfirst:
        nu = _sigmoid(v0_ref[...] + _dot(_dot(xv, v1_ref[...]), v2_ref[...]))
        v = v + (vfirst_ref[...] - v) * nu
    g_ref[...] = _dot(_sigmoid(_dot(xg, g1_ref[...])), g2_ref[...])
    r_ref[...] = r
    v_ref[...] = v
    a_ref[...] = a
    kk_ref[...] = k * vecs_ref[2:3]
    k_ref[...] = k * (1.0 + (a - 1.0) * vecs_ref[3:4])


def _rwkv_in(h, shift_init, v_first, gain, mix, w_in, w1, w2, a1, a2, g1, g2, v1, v2, v0, vecs, *, tm, shift,
             blocks_per_seq):
    n, d = h.shape
    sw = vecs.shape[1]
    n_seq = shift_init.shape[0]
    has_vfirst = v_first is not None
    row = lambda i: (i, 0)
    seq = lambda i: (i // blocks_per_seq, 0, 0)
    lora = lambda wa, wb: (_pad_cols(wa, LORA_PAD).astype(BF16), _pad_rows(wb, LORA_PAD).astype(BF16))
    w1p, w2p = lora(w1, w2)
    a1p, a2p = lora(a1, a2)
    g1p, g2p = lora(g1, g2)
    ins = [h, shift_init]
    specs = [pl.BlockSpec((tm, d), row), pl.BlockSpec((1, shift, d), seq)]
    if has_vfirst:
        ins.append(v_first)
        specs.append(pl.BlockSpec((tm, sw), row))
    consts = [gain.reshape(1, d), _pad_rows(mix, 8), w_in.astype(BF16), w1p, w2p, a1p, a2p, g1p, g2p]
    if has_vfirst:
        v1p, v2p = lora(v1, v2)
        consts += [v1p, v2p, v0.reshape(1, sw)]
    consts.append(_pad_rows(vecs, 8))
    ins += consts
    specs += [_const_spec(c.shape) for c in consts]
    wide = jax.ShapeDtypeStruct((n, sw), F32)
    outs = [wide] * 7 + [jax.ShapeDtypeStruct((n, MEM_WIDTH), BF16), jax.ShapeDtypeStruct((n_seq, shift, d), F32)]
    out_specs = [pl.BlockSpec((tm, sw), row)] * 7 + [pl.BlockSpec((tm, MEM_WIDTH), row),
                                                     pl.BlockSpec((1, shift, d), seq)]
    return pl.pallas_call(
        functools.partial(_rwkv_in_kernel, sw=sw, shift=shift, blocks_per_seq=blocks_per_seq, has_vfirst=has_vfirst),
        grid=(n // tm,),
        in_specs=specs, out_specs=out_specs, out_shape=outs,
        scratch_shapes=[pltpu.VMEM((shift, d), F32)],
        compiler_params=_params(1),
        name="rwkv_in",
    )(*ins)


def _wkv_kernel(*refs, chunk, n_chunks, heads, has_init):
    if has_init:
        (r_ref, lw_ref, k_ref, v_ref, kk_ref, a_ref, g_ref, vecs_ref, s0_ref, o_ref, sfin_ref, s_scr) = refs
    else:
        (r_ref, lw_ref, k_ref, v_ref, kk_ref, a_ref, g_ref, vecs_ref, o_ref, sfin_ref, s_scr) = refs
    tb = pl.program_id(2)
    c, nc, hd = chunk, n_chunks, HEAD_DIM
    n = c * nc

    @pl.when(tb == 0)
    def _():
        if has_init:
            s_scr[...] = s0_ref[0]
        else:
            s_scr[...] = jnp.zeros_like(s_scr)

    ri = lax.broadcasted_iota(jnp.int32, (n, n), 0)
    ci = lax.broadcasted_iota(jnp.int32, (n, n), 1)
    same = (ri // c) == (ci // c)
    incl = same & (ci <= ri)
    strict = same & (ci < ri)
    eye = (ci == ri).astype(F32)
    own = ((lax.broadcasted_iota(jnp.int32, (n, nc * hd), 0) // c)
           == (lax.broadcasted_iota(jnp.int32, (n, nc * hd), 1) // hd))
    n_double = max(c.bit_length() - 2, 0)

    def expand(x):
        return jnp.where(own, jnp.concatenate([x] * nc, axis=1), 0.0).astype(BF16)

    lw = lw_ref[...]
    cs = _cumsum_chunks(lw, c)
    g_inc_all = jnp.exp(cs)
    g_exc_all = jnp.exp(cs - lw)
    g_inv_all = jnp.exp(-cs)
    r_all, k_all, v_all = r_ref[...], k_ref[...], v_ref[...]
    kk_all, a_all, gate_all = kk_ref[...], a_ref[...], g_ref[...]

    hs = range(heads)
    sls = [slice(h * hd, (h + 1) * hd) for h in hs]
    r_ = [r_all[:, sl] for sl in sls]
    k_ = [k_all[:, sl] for sl in sls]
    v_ = [v_all[:, sl] for sl in sls]
    g_inc = [g_inc_all[:, sl] for sl in sls]
    kk_ = [kk_all[:, sl] for sl in sls]
    kk_ = [x / jnp.maximum(jnp.sqrt(jnp.sum(x * x, axis=-1, keepdims=True)), 1e-12) for x in kk_]
    rt = [r_[h] * g_inc[h] for h in hs]
    at = [-kk_[h] * g_exc_all[:, sls[h]] for h in hs]
    bt = [kk_[h] * a_all[:, sls[h]] * g_inv_all[:, sls[h]] for h in hs]
    kt = [k_[h] * g_inv_all[:, sls[h]] for h in hs]
    bk = [jnp.concatenate([bt[h], kt[h]], axis=0).astype(BF16) for h in hs]
    gram = [_dot_nt(jnp.concatenate([rt[h], at[h]], axis=0), bk[h]) for h in hs]
    m_rb = [jnp.where(incl, x[:n, :n], 0.0).astype(BF16) for x in gram]
    m_rk = [jnp.where(incl, x[:n, n:], 0.0).astype(BF16) for x in gram]
    m_ab = [jnp.where(strict, x[n:, :n], 0.0) for x in gram]
    m_ak = [jnp.where(strict, x[n:, n:], 0.0) for x in gram]
    inv = [eye + x for x in m_ab]
    p = m_ab
    for _ in range(n_double):
        p = [_dot(x, x) for x in p]
        inv = [inv[h] + _dot(inv[h], p[h]) for h in hs]
    inv = [x.astype(BF16) for x in inv]
    vb = [x.astype(BF16) for x in v_]
    w = [_dot(m_ak[h], vb[h]) for h in hs]
    a1 = [_dot(inv[h], at[h]) for h in hs]
    u = [_dot(inv[h], w[h]) for h in hs]
    r1 = [rt[h] + _dot(m_rb[h], a1[h]) for h in hs]
    y0 = [_dot(jnp.concatenate([m_rb[h], m_rk[h]], axis=1), jnp.concatenate([u[h].astype(BF16), vb[h]], axis=0))
          for h in hs]
    phi = [_dot_tn(expand(a1[h]), bt[h]) for h in hs]
    psi = [_dot_tn(jnp.concatenate([expand(u[h]), expand(v_[h])], axis=0), bk[h]) for h in hs]
    s = [s_scr[h] for h in hs]
    ys = [[] for _ in hs]
    for j in range(nc):
        rows = slice(j * c, (j + 1) * c)
        blk = slice(j * hd, (j + 1) * hd)
        for h in hs:
            ys[h].append(_dot_nt(r1[h][rows], s[h]) + y0[h][rows])
        s = [(s[h] + _dot(s[h], phi[h][blk]) + psi[h][blk]) * g_inc[h][(j + 1) * c - 1:(j + 1) * c] for h in hs]
    outs = []
    for h in hs:
        s_scr[h] = s[h]
        sl = sls[h]
        y = jnp.concatenate(ys[h], axis=0) if nc > 1 else ys[h][0]
        mu = jnp.mean(y, axis=-1, keepdims=True)
        var = jnp.mean(jnp.square(y - mu), axis=-1, keepdims=True)
        yn = (y - mu) * lax.rsqrt(var + GN_EPS) * vecs_ref[5:6, sl] + vecs_ref[6:7, sl]
        bonus = jnp.sum(r_[h] * k_[h] * vecs_ref[4:5, sl], axis=-1, keepdims=True) * v_[h]
        outs.append((yn + bonus) * gate_all[:, sl])
    o_ref[...] = jnp.concatenate(outs, axis=-1)

    @pl.when(tb == pl.num_programs(2) - 1)
    def _():
        sfin_ref[0] = s_scr[...]


def _wkv(r, lw, k, v, kk, a, g, vecs, s0, *, n_seq, seq_len, chunk, n_chunks):
    n, sw = r.shape
    n_heads = sw // HEAD_DIM
    width = WKV_HEADS * HEAD_DIM
    tb = chunk * n_chunks
    nb = seq_len // tb
    blk = pl.BlockSpec((tb, width), lambda b, hp, t: (b * nb + t, hp))
    ins = [r, lw, k, v, kk, a, g, _pad_rows(vecs, 8)]
    specs = [blk] * 7 + [pl.BlockSpec((8, width), lambda b, hp, t: (0, hp))]
    state_spec = pl.BlockSpec((1, WKV_HEADS, HEAD_DIM, HEAD_DIM), lambda b, hp, t: (b, hp, 0, 0))
    if s0 is not None:
        ins.append(s0)
        specs.append(state_spec)
    return pl.pallas_call(
        functools.partial(_wkv_kernel, chunk=chunk, n_chunks=n_chunks, heads=WKV_HEADS, has_init=s0 is not None),
        grid=(n_seq, n_heads // WKV_HEADS, nb),
        in_specs=specs,
        out_specs=[blk, state_spec],
        out_shape=[jax.ShapeDtypeStruct((n, sw), F32),
                   jax.ShapeDtypeStruct((n_seq, n_heads, HEAD_DIM, HEAD_DIM), F32)],
        scratch_shapes=[pltpu.VMEM((WKV_HEADS, HEAD_DIM, HEAD_DIM), F32)],
        compiler_params=_params(3),
        name="wkv",
    )(*ins)


def _mem_attn_kernel(q_ref, k_ref, v_ref, o_ref):
    q = q_ref[0]
    k = k_ref[0].astype(BF16)
    v = v_ref[0].astype(BF16)
    outs = []
    for h in range(H_MEM):
        sl = slice(h * HEAD_DIM, (h + 1) * HEAD_DIM)
        s = _dot_nt(q[:, sl], k[:, sl]) * ATTN_SCALE
        p = jnp.exp(s - jnp.max(s, axis=-1, keepdims=True))
        p = p / jnp.sum(p, axis=-1, keepdims=True)
        outs.append(_dot(p, v[:, sl]))
    o_ref[0] = jnp.concatenate(outs, axis=-1).astype(BF16)


def _mem_attn(cq, mem_k, mem_v, *, tq):
    n_seq, t, _ = cq.shape
    qspec = pl.BlockSpec((1, tq, MEM_WIDTH), lambda b, i: (b, i, 0))
    mspec = pl.BlockSpec((1, N_MEM, MEM_WIDTH), lambda b, i: (b, 0, 0))
    return pl.pallas_call(
        _mem_attn_kernel,
        grid=(n_seq, t // tq),
        in_specs=[qspec, mspec, mspec],
        out_specs=qspec,
        out_shape=jax.ShapeDtypeStruct(cq.shape, BF16),
        compiler_params=_params(2),
        name="mem_attn",
    )(cq, mem_k, mem_v)


def _out_proj_kernel(os_ref, om_ref, h_ref, w_ref, gain_ref, o_ref, *, sw):
    mixed = _dot(os_ref[...], w_ref[0:sw, :]) + _dot(om_ref[...], w_ref[sw:, :])
    o_ref[...] = h_ref[...] + _rms(mixed, gain_ref[...])


def _out_proj(o_self, o_mem, h, w_out, gain, *, tm):
    n, d = h.shape
    sw = o_self.shape[1]
    row = lambda i: (i, 0)
    return pl.pallas_call(
        functools.partial(_out_proj_kernel, sw=sw),
        grid=(n // tm,),
        in_specs=[pl.BlockSpec((tm, sw), row), pl.BlockSpec((tm, MEM_WIDTH), row), pl.BlockSpec((tm, d), row),
                  _const_spec(w_out.shape), _const_spec((1, d))],
        out_specs=pl.BlockSpec((tm, d), row),
        out_shape=jax.ShapeDtypeStruct((n, d), F32),
        compiler_params=_params(1),
        name="out_proj",
    )(o_self, o_mem, h, w_out.astype(BF16), gain.reshape(1, d))


def _gelu_tanh(x):
    return 0.5 * x * (1.0 + jnp.tanh(0.7978845608028654 * (x + 0.044715 * (x * x * x))))


def _ffn_kernel(h_ref, init_ref, gin_ref, gout_ref, wup_ref, cw_ref, cb_ref, wdn_ref, o_ref, conv_ref, carry_ref,
                acc_ref, *, d_ff, shift, blocks_per_seq, ff_chunk):
    @pl.when(pl.program_id(0) % blocks_per_seq == 0)
    def _():
        carry_ref[...] = init_ref[0]

    h = h_ref[...]
    tm = h.shape[0]
    xn = _rms(h, gin_ref[...]).astype(BF16)
    for j in range(d_ff // ff_chunk):
        cols = slice(j * ff_chunk, (j + 1) * ff_chunk)
        u = _dot(xn, wup_ref[:, cols])
        g = _dot(xn, wup_ref[:, d_ff + j * ff_chunk:d_ff + (j + 1) * ff_chunk])
        carry = carry_ref[:, cols]
        g1 = _shift_rows(g, carry, shift)
        g2 = _shift_rows(g1, carry[:shift], shift)
        gc = cb_ref[:, cols] + g2 * cw_ref[0:1, cols] + g1 * cw_ref[1:2, cols] + g * cw_ref[2:3, cols]
        carry_ref[:, cols] = g[tm - 2 * shift:]
        conv_ref[0, :, cols] = g[tm - 2 * shift:]
        part = _dot(_gelu_tanh(gc) * u, wdn_ref[cols, :])
        if j == 0:
            acc_ref[...] = part
        else:
            acc_ref[...] += part
    o_ref[...] = h + _rms(acc_ref[...], gout_ref[...])


def _ffn(h, conv_init, gain_in, gain_out, w_up, conv_w, conv_b, w_down, *, tm, shift, blocks_per_seq):
    n, d = h.shape
    d_ff = w_down.shape[0]
    n_seq = conv_init.shape[0]
    row = lambda i: (i, 0)
    seq = lambda i: (i // blocks_per_seq, 0, 0)
    once = pl.Buffered(1)
    return pl.pallas_call(
        functools.partial(_ffn_kernel, d_ff=d_ff, shift=shift, blocks_per_seq=blocks_per_seq, ff_chunk=2 * LANES),
        grid=(n // tm,),
        in_specs=[pl.BlockSpec((tm, d), row), pl.BlockSpec((1, 2 * shift, d_ff), seq),
                  _const_spec((1, d)), _const_spec((1, d)),
                  pl.BlockSpec(w_up.shape, lambda i: (0, 0), pipeline_mode=once),
                  _const_spec((8, d_ff)), _const_spec((1, d_ff)),
                  pl.BlockSpec(w_down.shape, lambda i: (0, 0), pipeline_mode=once)],
        out_specs=[pl.BlockSpec((tm, d), row), pl.BlockSpec((1, 2 * shift, d_ff), seq)],
        out_shape=[jax.ShapeDtypeStruct((n, d), F32), jax.ShapeDtypeStruct((n_seq, 2 * shift, d_ff), F32)],
        scratch_shapes=[pltpu.VMEM((2 * shift, d_ff), F32), pltpu.VMEM((tm, d), F32)],
        compiler_params=_params(1),
        name="ffn",
    )(h, conv_init, gain_in.reshape(1, d), gain_out.reshape(1, d), w_up.astype(BF16), _pad_rows(conv_w, 8),
      conv_b.reshape(1, d_ff), w_down.astype(BF16))


def _head_mean_sq(x, ones_bd):
    outs = []
    for j in range(x.shape[1] // 256):
        sq = jnp.square(x[:, j * 256:(j + 1) * 256])
        hi = sq.astype(BF16)
        lo = (sq - hi.astype(F32)).astype(BF16)
        outs.append(jnp.dot(hi, ones_bd, preferred_element_type=F32) + jnp.dot(lo, ones_bd, preferred_element_type=F32))
    return jnp.concatenate(outs, axis=-1) * (1.0 / HEAD_DIM)


def _feat_base(h):
    return FEAT * (h // 2) + HEAD_DIM * (1 - h % 2)


def _bias_features(c, n_heads):
    terms = jnp.concatenate(_split3(c * LOG2E), axis=1)
    r = lax.broadcasted_iota(jnp.int32, (3 * LANES, LANES), 0)
    l = lax.broadcasted_iota(jnp.int32, (3 * LANES, LANES), 1)
    head = r % LANES
    off = l - _feat_base(head)
    in_head = (off >= 0) & (off < FEAT) & (head < n_heads)
    place_q = (in_head & (off == r // LANES)).astype(BF16)
    place_k = (in_head & (off == 3 + r // LANES)).astype(BF16)
    lane = lax.broadcasted_iota(jnp.int32, (1, LANES), 1)
    slot = lane % FEAT
    used = ((lane % HEAD_DIM) // FEAT) < (n_heads // 2)
    ones_q = (used & (slot >= 3) & (slot < 6)).astype(F32)
    ones_k = (used & (slot < 3)).astype(F32)
    qf = jnp.dot(terms, place_q, preferred_element_type=F32) + ones_q
    kf = ones_k - jnp.dot(terms, place_k, preferred_element_type=F32)
    return qf.astype(BF16), kf.astype(BF16)


def _fox_in_kernel(h_ref, gain_ref, w_ref, bf_ref, qkg_ref, q_ref, k_ref, v_ref, kb_ref, vt_ref, lf_ref, c_ref,
                   qf_ref, kf_ref, og_ref, cq_ref, carry_ref, *, sw, n_heads, shift, steps, blocks_per_seq):
    xn = _rms(h_ref[...], gain_ref[...]).astype(BF16)
    tm = xn.shape[0]
    ri = lax.broadcasted_iota(jnp.int32, (256, 256), 0) // HEAD_DIM
    ci = lax.broadcasted_iota(jnp.int32, (256, 256), 1) // HEAD_DIM
    ones_bd = (ri == ci).astype(BF16)
    q = jnp.dot(xn, w_ref[:, 0:sw], preferred_element_type=F32)
    q = q * lax.rsqrt(_head_mean_sq(q, ones_bd) + NORM_EPS) * qkg_ref[0:1]
    q_ref[...] = (q * (ATTN_SCALE * LOG2E)).astype(BF16)
    k = jnp.dot(xn, w_ref[:, sw:2 * sw], preferred_element_type=F32)
    k = k * lax.rsqrt(_head_mean_sq(k, ones_bd) + NORM_EPS) * qkg_ref[1:2]
    k_ref[...] = k
    kb_ref[...] = k.astype(BF16)
    v = jnp.dot(xn, w_ref[:, 2 * sw:3 * sw], preferred_element_type=F32)
    v_ref[...] = v
    vt_ref[...] = v.T.astype(BF16)
    o0 = 3 * sw + LANES
    lf = -_softplus(-(jnp.dot(xn, w_ref[:, 3 * sw:o0], preferred_element_type=F32) + bf_ref[...]))
    lf_ref[...] = lf
    og_ref[...] = _sigmoid(jnp.dot(xn, w_ref[:, o0:o0 + sw], preferred_element_type=F32))
    cq_ref[...] = jnp.dot(xn, w_ref[:, o0 + sw:], preferred_element_type=F32).astype(BF16)
    if steps is None:
        @pl.when(pl.program_id(0) % blocks_per_seq == 0)
        def _():
            carry_ref[...] = jnp.zeros_like(carry_ref)

        c = _cumsum_chunks(lf, tm) + carry_ref[0:1]
        carry_ref[...] = jnp.broadcast_to(c[tm - 1:tm], carry_ref.shape)
    else:
        parts = [lf[0:shift]]
        for t in range(1, steps):
            parts.append(parts[-1] + lf[t * shift:(t + 1) * shift])
        c = jnp.concatenate(parts, axis=0)
    c_ref[...] = c
    qf, kf = _bias_features(c, n_heads)
    qf_ref[...] = qf
    kf_ref[...] = kf


def _fox_in(h, gain, w_in, bf, qk_gain, *, tm, shift, steps, blocks_per_seq):
    n, d = h.shape
    n_heads = bf.shape[0]
    sw = n_heads * HEAD_DIM
    w = jnp.concatenate([w_in[:, :3 * sw], _pad_cols(w_in[:, 3 * sw:3 * sw + n_heads], LANES),
                         w_in[:, 3 * sw + n_heads:]], axis=1).astype(BF16)
    qkg = _pad_rows(jnp.tile(qk_gain, (1, n_heads)), 8)
    row = lambda i: (i, 0)
    wide = pl.BlockSpec((tm, sw), row)
    lane = pl.BlockSpec((tm, LANES), row)
    wide_f32 = jax.ShapeDtypeStruct((n, sw), F32)
    wide_bf16 = jax.ShapeDtypeStruct((n, sw), BF16)
    lane_f32 = jax.ShapeDtypeStruct((n, LANES), F32)
    lane_bf16 = jax.ShapeDtypeStruct((n, LANES), BF16)
    return pl.pallas_call(
        functools.partial(_fox_in_kernel, sw=sw, n_heads=n_heads, shift=shift, steps=steps,
                          blocks_per_seq=blocks_per_seq),
        grid=(n // tm,),
        in_specs=[pl.BlockSpec((tm, d), row), _const_spec((1, d)), _const_spec(w.shape), _const_spec((1, LANES)),
                  _const_spec((8, sw))],
        out_specs=[wide, wide, wide, wide, pl.BlockSpec((sw, tm), lambda i: (0, i)), lane, lane, lane, lane, wide,
                   pl.BlockSpec((tm, MEM_WIDTH), row)],
        out_shape=[wide_bf16, wide_f32, wide_f32, wide_bf16, jax.ShapeDtypeStruct((sw, n), BF16), lane_f32, lane_f32,
                   lane_bf16, lane_bf16,
                   wide_f32, jax.ShapeDtypeStruct((n, MEM_WIDTH), BF16)],
        scratch_shapes=[pltpu.VMEM((8, LANES), F32)],
        compiler_params=_params(1),
        name="fox_in",
    )(h, gain.reshape(1, d), w, _pad_cols(bf.reshape(1, n_heads), LANES), qkg)


KEY_TILE = 128


def _fox_flash_kernel(q_ref, k_ref, vt_ref, qf_ref, kf_ref, og_ref, o_ref, m_ref, l_ref, acc_ref, *, n_heads, tq, tk):
    qi = pl.program_id(1)
    ki = pl.program_id(2)
    last = ((qi + 1) * tq - 1) // tk
    hd = HEAD_DIM
    hs = range(n_heads)

    @pl.when(ki == 0)
    def _():
        m_ref[...] = jnp.full_like(m_ref, NEG_BIG)
        l_ref[...] = jnp.zeros_like(l_ref)
        acc_ref[...] = jnp.zeros_like(acc_ref)

    lane = lax.broadcasted_iota(jnp.int32, (1, LANES), 1)
    upper = lane >= hd
    zero = jnp.zeros((), BF16)

    def step(masked):
        qf = qf_ref[...]
        kf = kf_ref[...]
        if masked:
            kpos = ki * tk + lax.broadcasted_iota(jnp.int32, (KEY_TILE, tq), 0)
            qpos = qi * tq + lax.broadcasted_iota(jnp.int32, (KEY_TILE, tq), 1)
        q2 = [q_ref[:, (h // 2) * LANES:(h // 2 + 1) * LANES] for h in hs]
        k2 = [k_ref[:, (h // 2) * LANES:(h // 2 + 1) * LANES] for h in hs]
        own = [upper == bool(h % 2) for h in hs]
        feat = [(lane >= _feat_base(h)) & (lane < _feat_base(h) + FEAT) for h in hs]
        q_aug = [jnp.where(own[h], q2[h], jnp.where(feat[h], qf, zero)) for h in hs]
        k_aug = [jnp.where(own[h], k2[h], kf) for h in hs]
        m = [m_ref[h:h + 1, :] for h in hs]
        l = [l_ref[h:h + 1, :] for h in hs]
        acc = [acc_ref[h * hd:(h + 1) * hd, :] for h in hs]
        for t in range(tk // KEY_TILE):
            rows = slice(t * KEY_TILE, (t + 1) * KEY_TILE)
            s = [_dot_nt(k_aug[h][rows], q_aug[h]) for h in hs]
            if masked:
                visible = (kpos + t * KEY_TILE) <= qpos
                s = [jnp.where(visible, x, NEG_BIG) for x in s]
            m_new = [jnp.maximum(m[h], jnp.max(s[h], axis=0, keepdims=True)) for h in hs]
            alpha = [jnp.exp2(m[h] - m_new[h]) for h in hs]
            p = [jnp.exp2(s[h] - m_new[h]) for h in hs]
            l = [alpha[h] * l[h] + jnp.sum(p[h], axis=0, keepdims=True) for h in hs]
            pv = [jnp.dot(vt_ref[h * hd:(h + 1) * hd, rows], p[h].astype(BF16), preferred_element_type=F32)
                  for h in hs]
            acc = [alpha[h] * acc[h] + pv[h] for h in hs]
            m = m_new
        for h in hs:
            m_ref[h:h + 1, :] = m[h]
            l_ref[h:h + 1, :] = l[h]
            acc_ref[h * hd:(h + 1) * hd, :] = acc[h]

    @pl.when(ki < last)
    def _():
        step(False)

    @pl.when(ki == last)
    def _():
        step(True)
        parts = [acc_ref[h * hd:(h + 1) * hd, :] * (1.0 / l_ref[h:h + 1, :]) for h in hs]
        o_ref[...] = jnp.concatenate(parts, axis=0).T * og_ref[...]


def _fox_flash(q, k, vt, qf, kf, og, *, n_seq, seq_len, tq, tk):
    n, sw = k.shape
    n_heads = sw // HEAD_DIM
    nq, nk = seq_len // tq, seq_len // tk
    kblk = lambda i, j: jnp.minimum(j, ((i + 1) * tq - 1) // tk)
    qrow = lambda b, i, j: (b * nq + i, 0)
    krow = lambda b, i, j: (b * nk + kblk(i, j), 0)
    return pl.pallas_call(
        functools.partial(_fox_flash_kernel, n_heads=n_heads, tq=tq, tk=tk),
        grid=(n_seq, nq, nk),
        in_specs=[pl.BlockSpec((tq, sw), qrow), pl.BlockSpec((tk, sw), krow),
                  pl.BlockSpec((sw, tk), lambda b, i, j: (0, b * nk + kblk(i, j))),
                  pl.BlockSpec((tq, LANES), qrow), pl.BlockSpec((tk, LANES), krow), pl.BlockSpec((tq, sw), qrow)],
        out_specs=pl.BlockSpec((tq, sw), qrow),
        out_shape=jax.ShapeDtypeStruct((n, sw), F32),
        scratch_shapes=[pltpu.VMEM((HEAD_ROWS, tq), F32), pltpu.VMEM((HEAD_ROWS, tq), F32),
                        pltpu.VMEM((sw, tq), F32)],
        compiler_params=_params(3),
        name="fox_flash",
    )(q, k, vt, qf, kf, og)


HEAD_ROWS = 16
PAGE_GROUP = 8


def _fox_paged_kernel(pt_ref, q_ref, cn_ref, kn_ref, vn_ref, og_ref, k_hbm, v_hbm, lp_hbm, o_ref,
                      kbuf, vbuf, lpbuf, sem, m_ref, l_ref, acc_ref, suf_ref, *, layer, n_heads, n_pages, group, page):
    i = pl.program_id(0)
    j = pl.program_id(1)
    n_groups = pl.num_programs(1)
    g = i * n_groups + j
    hd = HEAD_DIM
    hs = range(n_heads)

    def copies(seq, grp, slot):
        out = []
        for u in range(group):
            pg = pt_ref[seq, n_pages - 1 - (grp * group + u)]
            out.append(pltpu.make_async_copy(k_hbm.at[layer, pg], kbuf.at[slot, u], sem.at[slot, 0]))
            out.append(pltpu.make_async_copy(v_hbm.at[layer, pg], vbuf.at[slot, u], sem.at[slot, 1]))
            out.append(pltpu.make_async_copy(lp_hbm.at[layer, pg], lpbuf.at[slot, u, pl.ds(0, n_heads)], sem.at[slot, 2]))
        return out

    @pl.when(g == 0)
    def _():
        lpbuf[:, :, n_heads:, :] = jnp.zeros((2, group, HEAD_ROWS - n_heads, page), F32)
        for cp in copies(0, 0, 0):
            cp.start()

    @pl.when(g + 1 < pl.num_programs(0) * n_groups)
    def _():
        wrap = j + 1 == n_groups
        for cp in copies(jnp.where(wrap, i + 1, i), jnp.where(wrap, 0, j + 1), (g + 1) % 2):
            cp.start()

    slot = g % 2
    q = q_ref[0]
    q_ = [q[:, h * hd:(h + 1) * hd] for h in hs]

    @pl.when(j == 0)
    def _():
        kn, vn = kn_ref[0], vn_ref[0]
        steps = q.shape[0]
        trow = lax.broadcasted_iota(jnp.int32, (steps, steps), 0)
        tcol = lax.broadcasted_iota(jnp.int32, (steps, steps), 1)
        s = [jnp.where(tcol <= trow,
                       _dot_nt(q_[h], kn[:, h * hd:(h + 1) * hd]) + LOG2E * cn_ref[0, h][:, 0:steps], NEG_BIG)
             for h in hs]
        m = [jnp.max(x, axis=-1, keepdims=True) for x in s]
        p = [jnp.exp2(s[h] - m[h]) for h in hs]
        for h in hs:
            m_ref[h] = m[h]
            l_ref[h] = jnp.sum(p[h], axis=-1, keepdims=True)
            acc_ref[h] = _dot(p[h], vn[:, h * hd:(h + 1) * hd])
        suf_ref[...] = jnp.zeros_like(suf_ref)

    for cp in copies(i, j, slot):
        cp.wait()

    lp = [lpbuf[slot, u] for u in range(group)]
    ri = lax.broadcasted_iota(jnp.int32, (page, page), 0)
    ci = lax.broadcasted_iota(jnp.int32, (page, page), 1)
    later = (ri > ci).astype(BF16)
    hi, mid, lo = _split3(jnp.concatenate(lp, axis=0))
    within = (jnp.dot(hi, later, preferred_element_type=F32) + jnp.dot(mid, later, preferred_element_type=F32)
              + jnp.dot(lo, later, preferred_element_type=F32))
    carry = [suf_ref[...]]
    for u in range(group):
        carry.append(carry[-1] + jnp.sum(lp[u], axis=-1, keepdims=True))
    suf_ref[...] = carry[group]
    n_self = q.shape[0]
    bias = [jnp.concatenate([within[u * HEAD_ROWS + h:u * HEAD_ROWS + h + 1] + carry[u][h:h + 1]
                             for u in range(group)], axis=1) for h in hs]
    kcat = [jnp.concatenate([kbuf[slot, u, h] for u in range(group)], axis=1).astype(BF16) for h in hs]
    vcat = [jnp.concatenate([vbuf[slot, u, h] for u in range(group)], axis=1).astype(BF16) for h in hs]
    s = [_dot(q_[h], kcat[h]) + LOG2E * (bias[h] + cn_ref[0, h][:, n_self:n_self + 1]) for h in hs]
    m_old = [m_ref[h] for h in hs]
    m_new = [jnp.maximum(m_old[h], jnp.max(s[h], axis=-1, keepdims=True)) for h in hs]
    alpha = [jnp.exp2(m_old[h] - m_new[h]) for h in hs]
    p = [jnp.exp2(s[h] - m_new[h]) for h in hs]
    pv = [_dot_nt(p[h], vcat[h]) for h in hs]
    for h in hs:
        l_ref[h] = alpha[h] * l_ref[h] + jnp.sum(p[h], axis=-1, keepdims=True)
        acc_ref[h] = alpha[h] * acc_ref[h] + pv[h]
        m_ref[h] = m_new[h]

    @pl.when(j == n_groups - 1)
    def _():
        o_ref[0] = jnp.concatenate([acc_ref[h] / l_ref[h] for h in hs], axis=-1) * og_ref[0]


def _fox_paged(page_table, q, cn, k_new, v_new, og, cache_k, cache_v, cache_lp_t, *, layer, group):
    b, steps, sw = q.shape
    n_heads = sw // HEAD_DIM
    n_pages = page_table.shape[1]
    page = cache_k.shape[4]
    seq3 = lambda i, j, pt: (i, 0, 0)
    seq4 = lambda i, j, pt: (i, 0, 0, 0)
    tok = pl.BlockSpec((1, steps, sw), seq3)
    hbm = pl.BlockSpec(memory_space=pl.ANY)
    grid_spec = pltpu.PrefetchScalarGridSpec(
        num_scalar_prefetch=1,
        grid=(b, n_pages // group),
        in_specs=[tok, pl.BlockSpec((1, n_heads, steps, 2 * steps), seq4), tok, tok, tok, hbm, hbm, hbm],
        out_specs=tok,
        scratch_shapes=[pltpu.VMEM((2, group, n_heads, HEAD_DIM, page), F32),
                        pltpu.VMEM((2, group, n_heads, HEAD_DIM, page), F32),
                        pltpu.VMEM((2, group, HEAD_ROWS, page), F32),
                        pltpu.SemaphoreType.DMA((2, 3)),
                        pltpu.VMEM((n_heads, steps, 1), F32), pltpu.VMEM((n_heads, steps, 1), F32),
                        pltpu.VMEM((n_heads, steps, HEAD_DIM), F32), pltpu.VMEM((HEAD_ROWS, 1), F32)],
    )
    return pl.pallas_call(
        functools.partial(_fox_paged_kernel, layer=layer, n_heads=n_heads, n_pages=n_pages, group=group, page=page),
        grid_spec=grid_spec,
        out_shape=jax.ShapeDtypeStruct((b, steps, sw), F32),
        compiler_params=_params(2),
        name="fox_paged",
    )(page_table, q, cn, k_new, v_new, og, cache_k, cache_v, cache_lp_t)


def _to_time_major(x):
    x = jnp.swapaxes(x, 0, 1)
    return x.reshape((x.shape[0] * x.shape[1],) + x.shape[2:])


def _to_batch_major(x, b):
    return jnp.swapaxes(x.reshape((x.shape[0] // b, b) + x.shape[1:]), 0, 1)


def _trunk(h, P, *, prompt, n_seq, seq_len, mem_k, mem_v, rw_shift, rw_wkv, ffn_conv, fox_cache):
    depth = P["norm_gains"].shape[0]
    n, d = h.shape
    sw = P["rw_vecs"].shape[2]
    n_heads = sw // HEAD_DIM
    if prompt:
        tm, shift, blocks_per_seq = 256, 1, seq_len // 256
    else:
        tm, shift, blocks_per_seq = n, n_seq, 1
    v_first = None
    shifts, wkvs, fks, fvs, fls, convs = [], [], [], [], [], []
    for l in range(depth):
        gains = P["norm_gains"][l]
        i = l // 2
        if l % 2 == 0:
            init = jnp.zeros((n_seq, 1, d), F32) if prompt else rw_shift[i][None]
            vj = None if i == 0 else i - 1
            r, lw, k, v, kk, a, g, cq, sh = _rwkv_in(
                h, init, v_first, gains[0], P["rw_mix"][i], P["rw_w_in"][i], P["rw_w1"][i], P["rw_w2"][i],
                P["rw_a1"][i], P["rw_a2"][i], P["rw_g1"][i], P["rw_g2"][i],
                None if vj is None else P["rw_v1"][vj], None if vj is None else P["rw_v2"][vj],
                None if vj is None else P["rw_v0"][vj], P["rw_vecs"][i],
                tm=tm, shift=shift, blocks_per_seq=blocks_per_seq)
            if v_first is None:
                v_first = v
            if prompt:
                o_self, s_fin = _wkv(r, lw, k, v, kk, a, g, P["rw_vecs"][i], None, n_seq=n_seq, seq_len=seq_len,
                                     chunk=WKV_CHUNK, n_chunks=4)
                shifts.append(sh[:, 0])
            else:
                pad = lambda x: jnp.pad(_to_batch_major(x, n_seq), ((0, 0), (0, 8 - seq_len), (0, 0))).reshape(n_seq * 8, sw)
                o_pad, s_fin = _wkv(*[pad(x) for x in (r, lw, k, v, kk, a, g)], P["rw_vecs"][i], rw_wkv[i],
                                    n_seq=n_seq, seq_len=8, chunk=8, n_chunks=1)
                o_self = _to_time_major(o_pad.reshape(n_seq, 8, sw)[:, :seq_len])
                shifts.append(sh[0])
            wkvs.append(s_fin)
        else:
            q, k, v, kb, vt, lf, c, qf, kf, og, cq = _fox_in(
                h, gains[0], P["fox_w_in"][i], P["fox_bf"][i], P["fox_qk_gain"][i],
                tm=tm, shift=shift, steps=None if prompt else seq_len, blocks_per_seq=blocks_per_seq)
            if prompt:
                o_self = _fox_flash(q, kb, vt, qf, kf, og, n_seq=n_seq, seq_len=seq_len, tq=256,
                                    tk=min(512, seq_len))
            else:
                cache_k, cache_v, cache_lp_t, page_table = fox_cache
                pad8 = lambda x: jnp.pad(_to_batch_major(x, n_seq), ((0, 0), (0, 8 - seq_len), (0, 0)))
                ch = jnp.swapaxes(pad8(c)[:, :, :n_heads], 1, 2)
                cn = jnp.concatenate([ch[:, :, :, None] - ch[:, :, None, :],
                                      jnp.broadcast_to(ch[..., None], ch.shape + (8,))], axis=-1)
                o_b = _fox_paged(page_table, pad8(q), cn, pad8(k), pad8(v), pad8(og), cache_k, cache_v, cache_lp_t,
                                 layer=i, group=min(PAGE_GROUP, page_table.shape[1]))
                o_self = _to_time_major(o_b[:, :seq_len])
            fks.append(k)
            fvs.append(v)
            fls.append(lf[:, :n_heads])
        if prompt:
            o_mem = _mem_attn(cq.reshape(n_seq, seq_len, MEM_WIDTH), mem_k[l], mem_v[l],
                              tq=min(1024, seq_len)).reshape(n, MEM_WIDTH)
        else:
            cq_b = jnp.pad(_to_batch_major(cq, n_seq), ((0, 0), (0, 16 - seq_len), (0, 0)))
            o_mem = _to_time_major(_mem_attn(cq_b, mem_k[l], mem_v[l], tq=16)[:, :seq_len])
        h = _out_proj(o_self, o_mem, h, P["w_out"][l], gains[1], tm=tm)
        if prompt:
            conv_init = jnp.zeros((n_seq, CONV_W - 1, P["ffn_conv_b"].shape[1]), F32)
        else:
            conv_init = _to_time_major(ffn_conv[l])[None]
        h, cs = _ffn(h, conv_init, gains[2], gains[3], P["ffn_w_up"][l], P["ffn_conv_w"][l], P["ffn_conv_b"][l],
                     P["ffn_w_down"][l], tm=tm, shift=shift, blocks_per_seq=blocks_per_seq)
        convs.append(cs if prompt else _to_batch_major(cs[0], n_seq))
    return h, wkvs, shifts, fks, fvs, fls, convs


def kernel(x_prompt, x_sample, cache_mem_k, cache_mem_v, cache_fox_k, cache_fox_v, cache_fox_logf, state_rwkv_wkv,
           state_rwkv_shift, state_ffn_conv, page_table, mem_prompt, norm_gains, w_out, mem_norm, w_mem_kv, rw_mix,
           rw_w_in, rw_vecs, rw_w1, rw_w2, rw_a1, rw_a2, rw_g1, rw_g2, rw_v0, rw_v1, rw_v2, fox_w_in, fox_bf,
           fox_qk_gain, ffn_w_up, ffn_conv_w, ffn_conv_b, ffn_w_down):
    P = dict(norm_gains=norm_gains, w_out=w_out, rw_mix=rw_mix, rw_w_in=rw_w_in, rw_vecs=rw_vecs, rw_w1=rw_w1,
             rw_w2=rw_w2, rw_a1=rw_a1, rw_a2=rw_a2, rw_g1=rw_g1, rw_g2=rw_g2, rw_v0=rw_v0, rw_v1=rw_v1, rw_v2=rw_v2,
             fox_w_in=fox_w_in, fox_bf=fox_bf, fox_qk_gain=fox_qk_gain, ffn_w_up=ffn_w_up, ffn_conv_w=ffn_conv_w,
             ffn_conv_b=ffn_conv_b, ffn_w_down=ffn_w_down)
    b, s, d = x_prompt.shape
    db, ds, _ = x_sample.shape
    depth = norm_gains.shape[0]
    n_heads = fox_bf.shape[1]
    sw = n_heads * HEAD_DIM

    p_mem_k, p_mem_v = _mem_kv(mem_prompt, mem_norm, w_mem_kv)
    hp, p_wkv, p_shift, p_fk, p_fv, p_fl, p_conv = _trunk(
        x_prompt.reshape(b * s, d), P, prompt=True, n_seq=b, seq_len=s, mem_k=p_mem_k, mem_v=p_mem_v,
        rw_shift=None, rw_wkv=None, ffn_conv=None, fox_cache=None)

    cache_k = jnp.transpose(cache_fox_k, (0, 1, 3, 4, 2))
    cache_v = jnp.transpose(cache_fox_v, (0, 1, 3, 4, 2))
    cache_lp_t = jnp.swapaxes(cache_fox_logf, 2, 3)
    hs, s_wkv, s_shift, s_fk, s_fv, s_fl, s_conv = _trunk(
        _to_time_major(x_sample), P, prompt=False, n_seq=db, seq_len=ds,
        mem_k=cache_mem_k.reshape(depth, db, N_MEM, MEM_WIDTH), mem_v=cache_mem_v.reshape(depth, db, N_MEM, MEM_WIDTH),
        rw_shift=state_rwkv_shift, rw_wkv=state_rwkv_wkv, ffn_conv=state_ffn_conv,
        fox_cache=(cache_k, cache_v, cache_lp_t, page_table))

    heads = lambda x, bb, tt: x.reshape(bb, tt, n_heads, HEAD_DIM)
    mem_heads = lambda x: x.reshape(depth, b, N_MEM, H_MEM, HEAD_DIM)
    return (hp.reshape(b, s, d), _to_batch_major(hs, db),
            jnp.stack(p_wkv), jnp.stack(p_shift),
            jnp.stack([heads(x, b, s) for x in p_fk]), jnp.stack([heads(x, b, s) for x in p_fv]),
            jnp.stack([x.reshape(b, s, n_heads) for x in p_fl]),
            mem_heads(p_mem_k), mem_heads(p_mem_v), jnp.stack(p_conv),
            jnp.stack(s_wkv), jnp.stack(s_shift),
            jnp.stack([heads(_to_batch_major(x, db), db, ds) for x in s_fk]),
            jnp.stack([heads(_to_batch_major(x, db), db, ds) for x in s_fv]),
            jnp.stack([_to_batch_major(x, db) for x in s_fl]),
            jnp.stack(s_conv))
```

```python
import functools

import jax
import jax.numpy as jnp
from jax import lax
from jax.experimental import pallas as pl
from jax.experimental.pallas import tpu as pltpu

F32 = jnp.float32
BF16 = jnp.bfloat16

HEAD_DIM = 64
N_MEM = 256
H_MEM = 4
MEM_WIDTH = H_MEM * HEAD_DIM
NORM_EPS = 1e-6
GN_EPS = 64e-5
ATTN_SCALE = HEAD_DIM ** -0.5
LOG2E = 1.4426950408889634
CONV_W = 3
LANES = 128
LORA_PAD = 128
VMEM_LIMIT = 56 * 1024 * 1024
NEG_BIG = -1e30
WKV_CHUNK = 64
WKV_HEADS = 6
FEAT = 8


def _params(n_axes):
    return pltpu.CompilerParams(dimension_semantics=("arbitrary",) * n_axes, vmem_limit_bytes=VMEM_LIMIT)


def _rms(x, g):
    return x * lax.rsqrt(jnp.mean(x * x, axis=-1, keepdims=True) + NORM_EPS) * g


def _dot(a, b):
    return jnp.dot(a.astype(BF16), b.astype(BF16), preferred_element_type=F32)


def _dot_nt(a, b):
    return lax.dot_general(a.astype(BF16), b.astype(BF16), (((1,), (1,)), ((), ())), preferred_element_type=F32)


def _dot_tn(a, b):
    return lax.dot_general(a.astype(BF16), b.astype(BF16), (((0,), (0,)), ((), ())), preferred_element_type=F32)


def _sigmoid(x):
    return 1.0 / (1.0 + jnp.exp(-x))


def _softplus(x):
    return jnp.maximum(x, 0.0) + jnp.log(1.0 + jnp.exp(-jnp.abs(x)))


def _split3(x):
    hi = x.astype(BF16)
    r1 = x - hi.astype(F32)
    mid = r1.astype(BF16)
    lo = (r1 - mid.astype(F32)).astype(BF16)
    return hi, mid, lo


def _shift_rows(x, carry, shift):
    n = x.shape[0]
    c = carry.shape[0]
    if shift % 8 == 0:
        return jnp.concatenate([carry[c - shift:], x[:n - shift]], axis=0)
    row = lax.broadcasted_iota(jnp.int32, x.shape, 0)
    out = pltpu.roll(x, shift, axis=0)
    for j in range(shift):
        out = jnp.where(row == j, carry[c - shift + j:c - shift + j + 1], out)
    return out


def _cumsum_chunks(x, c):
    pos = lax.broadcasted_iota(jnp.int32, x.shape, 0) & (c - 1)
    s = 1
    while s < c:
        x = x + jnp.where(pos >= s, pltpu.roll(x, s, axis=0), 0.0)
        s *= 2
    return x


def _const_spec(shape):
    nd = len(shape)
    return pl.BlockSpec(shape, lambda *_: (0,) * nd)


def _pad_cols(w, n):
    return jnp.pad(w, ((0, 0), (0, n - w.shape[1])))


def _pad_rows(w, n):
    return jnp.pad(w, ((0, n - w.shape[0]), (0, 0)))


def _memkv_kernel(mem_ref, gain_ref, w_ref, k_ref, v_ref):
    xn = _rms(mem_ref[0], gain_ref[0])
    kv = _dot(xn, w_ref[0])
    k_ref[0, 0] = kv[:, :MEM_WIDTH]
    v_ref[0, 0] = kv[:, MEM_WIDTH:]


def _mem_kv(mem, mem_norm, w_mem_kv):
    depth, d = mem_norm.shape
    b = mem.shape[0]
    out = jax.ShapeDtypeStruct((depth, b, N_MEM, MEM_WIDTH), F32)
    return pl.pallas_call(
        _memkv_kernel,
        grid=(depth, b),
        in_specs=[pl.BlockSpec((1, N_MEM, d), lambda l, i: (i, 0, 0)),
                  pl.BlockSpec((1, 1, d), lambda l, i: (l, 0, 0)),
                  pl.BlockSpec((1, d, 2 * MEM_WIDTH), lambda l, i: (l, 0, 0))],
        out_specs=[pl.BlockSpec((1, 1, N_MEM, MEM_WIDTH), lambda l, i: (l, i, 0, 0))] * 2,
        out_shape=[out, out],
        compiler_params=_params(2),
        name="mem_kv",
    )(mem, mem_norm.reshape(depth, 1, d), w_mem_kv.astype(BF16))


def _rwkv_in_kernel(*refs, sw, shift, blocks_per_seq, has_vfirst):
    if has_vfirst:
        (h_ref, init_ref, vfirst_ref, gain_ref, mix_ref, w_in_ref, w1_ref, w2_ref, a1_ref, a2_ref, g1_ref, g2_ref,
         v1_ref, v2_ref, v0_ref, vecs_ref,
         r_ref, lw_ref, k_ref, v_ref, kk_ref, a_ref, g_ref, cq_ref, shift_ref, carry_ref) = refs
    else:
        (h_ref, init_ref, gain_ref, mix_ref, w_in_ref, w1_ref, w2_ref, a1_ref, a2_ref, g1_ref, g2_ref, vecs_ref,
         r_ref, lw_ref, k_ref, v_ref, kk_ref, a_ref, g_ref, cq_ref, shift_ref, carry_ref) = refs

    @pl.when(pl.program_id(0) % blocks_per_seq == 0)
    def _():
        carry_ref[...] = init_ref[0]

    xn = _rms(h_ref[...], gain_ref[...])
    tm = xn.shape[0]
    prev = _shift_rows(xn, carry_ref[...], shift)
    carry_ref[...] = xn[tm - shift:]
    shift_ref[0] = xn[tm - shift:]
    xx = prev - xn
    xr, xw, xk, xv, xa, xg = [(xn + xx * mix_ref[j:j + 1]).astype(BF16) for j in range(6)]
    r = _dot(xr, w_in_ref[:, 0:sw])
    k = _dot(xk, w_in_ref[:, sw:2 * sw])
    v = _dot(xv, w_in_ref[:, 2 * sw:3 * sw])
    cq_ref[...] = _dot(xn, w_in_ref[:, 3 * sw:]).astype(BF16)
    w_pre = vecs_ref[0:1] + _dot(jnp.tanh(_dot(xw, w1_ref[...])), w2_ref[...])
    lw_ref[...] = -jnp.exp(-_softplus(-w_pre) - 0.5)
    a = _sigmoid(vecs_ref[1:2] + _dot(_dot(xa, a1_ref[...]), a2_ref[...]))
    if has_vfirst:
        nu = _sigmoid(v0_ref[...] + _dot(_dot(xv, v1_ref[...]), v2_ref[...]))
        v = v + (vfirst_ref[...] - v) * nu
    g_ref[...] = _dot(_sigmoid(_dot(xg, g1_ref[...])), g2_ref[...])
    r_ref[...] = r
    v_ref[...] = v
    a_ref[...] = a
    kk_ref[...] = k * vecs_ref[2:3]
    k_ref[...] = k * (1.0 + (a - 1.0) * vecs_ref[3:4])


def _rwkv_in(h, shift_init, v_first, gain, mix, w_in, w1, w2, a1, a2, g1, g2, v1, v2, v0, vecs, *, tm, shift,
             blocks_per_seq):
    n, d = h.shape
    sw = vecs.shape[1]
    n_seq = shift_init.shape[0]
    has_vfirst = v_first is not None
    row = lambda i: (i, 0)
    seq = lambda i: (i // blocks_per_seq, 0, 0)
    lora = lambda wa, wb: (_pad_cols(wa, LORA_PAD).astype(BF16), _pad_rows(wb, LORA_PAD).astype(BF16))
    w1p, w2p = lora(w1, w2)
    a1p, a2p = lora(a1, a2)
    g1p, g2p = lora(g1, g2)
    ins = [h, shift_init]
    specs = [pl.BlockSpec((tm, d), row), pl.BlockSpec((1, shift, d), seq)]
    if has_vfirst:
        ins.append(v_first)
        specs.append(pl.BlockSpec((tm, sw), row))
    consts = [gain.reshape(1, d), _pad_rows(mix, 8), w_in.astype(BF16), w1p, w2p, a1p, a2p, g1p, g2p]
    if has_vfirst:
        v1p, v2p = lora(v1, v2)
        consts += [v1p, v2p, v0.reshape(1, sw)]
    consts.append(_pad_rows(vecs, 8))
    ins += consts
    specs += [_const_spec(c.shape) for c in consts]
    wide = jax.ShapeDtypeStruct((n, sw), F32)
    outs = [wide] * 7 + [jax.ShapeDtypeStruct((n, MEM_WIDTH), BF16), jax.ShapeDtypeStruct((n_seq, shift, d), F32)]
    out_specs = [pl.BlockSpec((tm, sw), row)] * 7 + [pl.BlockSpec((tm, MEM_WIDTH), row),
                                                     pl.BlockSpec((1, shift, d), seq)]
    return pl.pallas_call(
        functools.partial(_rwkv_in_kernel, sw=sw, shift=shift, blocks_per_seq=blocks_per_seq, has_vfirst=has_vfirst),
        grid=(n // tm,),
        in_specs=specs, out_specs=out_specs, out_shape=outs,
        scratch_shapes=[pltpu.VMEM((shift, d), F32)],
        compiler_params=_params(1),
        name="rwkv_in",
    )(*ins)


def _wkv_kernel(*refs, chunk, n_chunks, heads, has_init):
    if has_init:
        (r_ref, lw_ref, k_ref, v_ref, kk_ref, a_ref, g_ref, vecs_ref, s0_ref, o_ref, sfin_ref, s_scr) = refs
    else:
        (r_ref, lw_ref, k_ref, v_ref, kk_ref, a_ref, g_ref, vecs_ref, o_ref, sfin_ref, s_scr) = refs
    tb = pl.program_id(2)
    c, nc, hd = chunk, n_chunks, HEAD_DIM
    n = c * nc

    @pl.when(tb == 0)
    def _():
        if has_init:
            s_scr[...] = s0_ref[0]
        else:
            s_scr[...] = jnp.zeros_like(s_scr)

    ri = lax.broadcasted_iota(jnp.int32, (n, n), 0)
    ci = lax.broadcasted_iota(jnp.int32, (n, n), 1)
    same = (ri // c) == (ci // c)
    incl = same & (ci <= ri)
    strict = same & (ci < ri)
    eye = (ci == ri).astype(F32)
    own = ((lax.broadcasted_iota(jnp.int32, (n, nc * hd), 0) // c)
           == (lax.broadcasted_iota(jnp.int32, (n, nc * hd), 1) // hd))
    n_double = max(c.bit_length() - 2, 0)

    def expand(x):
        return jnp.where(own, jnp.concatenate([x] * nc, axis=1), 0.0).astype(BF16)

    lw = lw_ref[...]
    cs = _cumsum_chunks(lw, c)
    g_inc_all = jnp.exp(cs)
    g_exc_all = jnp.exp(cs - lw)
    g_inv_all = jnp.exp(-cs)
    r_all, k_all, v_all = r_ref[...], k_ref[...], v_ref[...]
    kk_all, a_all, gate_all = kk_ref[...], a_ref[...], g_ref[...]

    hs = range(heads)
    sls = [slice(h * hd, (h + 1) * hd) for h in hs]
    r_ = [r_all[:, sl] for sl in sls]
    k_ = [k_all[:, sl] for sl in sls]
    v_ = [v_all[:, sl] for sl in sls]
    g_inc = [g_inc_all[:, sl] for sl in sls]
    kk_ = [kk_all[:, sl] for sl in sls]
    kk_ = [x / jnp.maximum(jnp.sqrt(jnp.sum(x * x, axis=-1, keepdims=True)), 1e-12) for x in kk_]
    rt = [r_[h] * g_inc[h] for h in hs]
    at = [-kk_[h] * g_exc_all[:, sls[h]] for h in hs]
    bt = [kk_[h] * a_all[:, sls[h]] * g_inv_all[:, sls[h]] for h in hs]
    kt = [k_[h] * g_inv_all[:, sls[h]] for h in hs]
    bk = [jnp.concatenate([bt[h], kt[h]], axis=0).astype(BF16) for h in hs]
    gram = [_dot_nt(jnp.concatenate([rt[h], at[h]], axis=0), bk[h]) for h in hs]
    m_rb = [jnp.where(incl, x[:n, :n], 0.0).astype(BF16) for x in gram]
    m_rk = [jnp.where(incl, x[:n, n:], 0.0).astype(BF16) for x in gram]
    m_ab = [jnp.where(strict, x[n:, :n], 0.0) for x in gram]
    m_ak = [jnp.where(strict, x[n:, n:], 0.0) for x in gram]
    inv = [eye + x for x in m_ab]
    p = m_ab
    for _ in range(n_double):
        p = [_dot(x, x) for x in p]
        inv = [inv[h] + _dot(inv[h], p[h]) for h in hs]
    inv = [x.astype(BF16) for x in inv]
    vb = [x.astype(BF16) for x in v_]
    w = [_dot(m_ak[h], vb[h]) for h in hs]
    a1 = [_dot(inv[h], at[h]) for h in hs]
    u = [_dot(inv[h], w[h]) for h in hs]
    r1 = [rt[h] + _dot(m_rb[h], a1[h]) for h in hs]
    y0 = [_dot(jnp.concatenate([m_rb[h], m_rk[h]], axis=1), jnp.concatenate([u[h].astype(BF16), vb[h]], axis=0))
          for h in hs]
    phi = [_dot_tn(expand(a1[h]), bt[h]) for h in hs]
    psi = [_dot_tn(jnp.concatenate([expand(u[h]), expand(v_[h])], axis=0), bk[h]) for h in hs]
    s = [s_scr[h] for h in hs]
    ys = [[] for _ in hs]
    for j in range(nc):
        rows = slice(j * c, (j + 1) * c)
        blk = slice(j * hd, (j + 1) * hd)
        for h in hs:
            ys[h].append(_dot_nt(r1[h][rows], s[h]) + y0[h][rows])
        s = [(s[h] + _dot(s[h], phi[h][blk]) + psi[h][blk]) * g_inc[h][(j + 1) * c - 1:(j + 1) * c] for h in hs]
    outs = []
    for h in hs:
        s_scr[h] = s[h]
        sl = sls[h]
        y = jnp.concatenate(ys[h], axis=0) if nc > 1 else ys[h][0]
        mu = jnp.mean(y, axis=-1, keepdims=True)
        var = jnp.mean(jnp.square(y - mu), axis=-1, keepdims=True)
        yn = (y - mu) * lax.rsqrt(var + GN_EPS) * vecs_ref[5:6, sl] + vecs_ref[6:7, sl]
        bonus = jnp.sum(r_[h] * k_[h] * vecs_ref[4:5, sl], axis=-1, keepdims=True) * v_[h]
        outs.append((yn + bonus) * gate_all[:, sl])
    o_ref[...] = jnp.concatenate(outs, axis=-1)

    @pl.when(tb == pl.num_programs(2) - 1)
    def _():
        sfin_ref[0] = s_scr[...]


def _wkv(r, lw, k, v, kk, a, g, vecs, s0, *, n_seq, seq_len, chunk, n_chunks):
    n, sw = r.shape
    n_heads = sw // HEAD_DIM
    width = WKV_HEADS * HEAD_DIM
    tb = chunk * n_chunks
    nb = seq_len // tb
    blk = pl.BlockSpec((tb, width), lambda b, hp, t: (b * nb + t, hp))
    ins = [r, lw, k, v, kk, a, g, _pad_rows(vecs, 8)]
    specs = [blk] * 7 + [pl.BlockSpec((8, width), lambda b, hp, t: (0, hp))]
    state_spec = pl.BlockSpec((1, WKV_HEADS, HEAD_DIM, HEAD_DIM), lambda b, hp, t: (b, hp, 0, 0))
    if s0 is not None:
        ins.append(s0)
        specs.append(state_spec)
    return pl.pallas_call(
        functools.partial(_wkv_kernel, chunk=chunk, n_chunks=n_chunks, heads=WKV_HEADS, has_init=s0 is not None),
        grid=(n_seq, n_heads // WKV_HEADS, nb),
        in_specs=specs,
        out_specs=[blk, state_spec],
        out_shape=[jax.ShapeDtypeStruct((n, sw), F32),
                   jax.ShapeDtypeStruct((n_seq, n_heads, HEAD_DIM, HEAD_DIM), F32)],
        scratch_shapes=[pltpu.VMEM((WKV_HEADS, HEAD_DIM, HEAD_DIM), F32)],
        compiler_params=_params(3),
        name="wkv",
    )(*ins)


def _mem_attn_kernel(q_ref, k_ref, v_ref, o_ref):
    q = q_ref[0]
    k = k_ref[0].astype(BF16)
    v = v_ref[0].astype(BF16)
    outs = []
    for h in range(H_MEM):
        sl = slice(h * HEAD_DIM, (h + 1) * HEAD_DIM)
        s = _dot_nt(q[:, sl], k[:, sl]) * ATTN_SCALE
        p = jnp.exp(s - jnp.max(s, axis=-1, keepdims=True))
        p = p / jnp.sum(p, axis=-1, keepdims=True)
        outs.append(_dot(p, v[:, sl]))
    o_ref[0] = jnp.concatenate(outs, axis=-1).astype(BF16)


def _mem_attn(cq, mem_k, mem_v, *, tq):
    n_seq, t, _ = cq.shape
    qspec = pl.BlockSpec((1, tq, MEM_WIDTH), lambda b, i: (b, i, 0))
    mspec = pl.BlockSpec((1, N_MEM, MEM_WIDTH), lambda b, i: (b, 0, 0))
    return pl.pallas_call(
        _mem_attn_kernel,
        grid=(n_seq, t // tq),
        in_specs=[qspec, mspec, mspec],
        out_specs=qspec,
        out_shape=jax.ShapeDtypeStruct(cq.shape, BF16),
        compiler_params=_params(2),
        name="mem_attn",
    )(cq, mem_k, mem_v)


def _out_proj_kernel(os_ref, om_ref, h_ref, w_ref, gain_ref, o_ref, *, sw):
    mixed = _dot(os_ref[...], w_ref[0:sw, :]) + _dot(om_ref[...], w_ref[sw:, :])
    o_ref[...] = h_ref[...] + _rms(mixed, gain_ref[...])


def _out_proj(o_self, o_mem, h, w_out, gain, *, tm):
    n, d = h.shape
    sw = o_self.shape[1]
    row = lambda i: (i, 0)
    return pl.pallas_call(
        functools.partial(_out_proj_kernel, sw=sw),
        grid=(n // tm,),
        in_specs=[pl.BlockSpec((tm, sw), row), pl.BlockSpec((tm, MEM_WIDTH), row), pl.BlockSpec((tm, d), row),
                  _const_spec(w_out.shape), _const_spec((1, d))],
        out_specs=pl.BlockSpec((tm, d), row),
        out_shape=jax.ShapeDtypeStruct((n, d), F32),
        compiler_params=_params(1),
        name="out_proj",
    )(o_self, o_mem, h, w_out.astype(BF16), gain.reshape(1, d))


def _gelu_tanh(x):
    return 0.5 * x * (1.0 + jnp.tanh(0.7978845608028654 * (x + 0.044715 * (x * x * x))))


def _ffn_kernel(h_ref, init_ref, gin_ref, gout_ref, wup_ref, cw_ref, cb_ref, wdn_ref, o_ref, conv_ref, carry_ref,
                acc_ref, *, d_ff, shift, blocks_per_seq, ff_chunk):
    @pl.when(pl.program_id(0) % blocks_per_seq == 0)
    def _():
        carry_ref[...] = init_ref[0]

    h = h_ref[...]
    tm = h.shape[0]
    xn = _rms(h, gin_ref[...]).astype(BF16)
    for j in range(d_ff // ff_chunk):
        cols = slice(j * ff_chunk, (j + 1) * ff_chunk)
        u = _dot(xn, wup_ref[:, cols])
        g = _dot(xn, wup_ref[:, d_ff + j * ff_chunk:d_ff + (j + 1) * ff_chunk])
        carry = carry_ref[:, cols]
        g1 = _shift_rows(g, carry, shift)
        g2 = _shift_rows(g1, carry[:shift], shift)
        gc = cb_ref[:, cols] + g2 * cw_ref[0:1, cols] + g1 * cw_ref[1:2, cols] + g * cw_ref[2:3, cols]
        carry_ref[:, cols] = g[tm - 2 * shift:]
        conv_ref[0, :, cols] = g[tm - 2 * shift:]
        part = _dot(_gelu_tanh(gc) * u, wdn_ref[cols, :])
        if j == 0:
            acc_ref[...] = part
        else:
            acc_ref[...] += part
    o_ref[...] = h + _rms(acc_ref[...], gout_ref[...])


def _ffn(h, conv_init, gain_in, gain_out, w_up, conv_w, conv_b, w_down, *, tm, shift, blocks_per_seq):
    n, d = h.shape
    d_ff = w_down.shape[0]
    n_seq = conv_init.shape[0]
    row = lambda i: (i, 0)
    seq = lambda i: (i // blocks_per_seq, 0, 0)
    once = pl.Buffered(1)
    return pl.pallas_call(
        functools.partial(_ffn_kernel, d_ff=d_ff, shift=shift, blocks_per_seq=blocks_per_seq, ff_chunk=2 * LANES),
        grid=(n // tm,),
        in_specs=[pl.BlockSpec((tm, d), row), pl.BlockSpec((1, 2 * shift, d_ff), seq),
                  _const_spec((1, d)), _const_spec((1, d)),
                  pl.BlockSpec(w_up.shape, lambda i: (0, 0), pipeline_mode=once),
                  _const_spec((8, d_ff)), _const_spec((1, d_ff)),
                  pl.BlockSpec(w_down.shape, lambda i: (0, 0), pipeline_mode=once)],
        out_specs=[pl.BlockSpec((tm, d), row), pl.BlockSpec((1, 2 * shift, d_ff), seq)],
        out_shape=[jax.ShapeDtypeStruct((n, d), F32), jax.ShapeDtypeStruct((n_seq, 2 * shift, d_ff), F32)],
        scratch_shapes=[pltpu.VMEM((2 * shift, d_ff), F32), pltpu.VMEM((tm, d), F32)],
        compiler_params=_params(1),
        name="ffn",
    )(h, conv_init, gain_in.reshape(1, d), gain_out.reshape(1, d), w_up.astype(BF16), _pad_rows(conv_w, 8),
      conv_b.reshape(1, d_ff), w_down.astype(BF16))


def _head_mean_sq(x, ones_bd):
    outs = []
    for j in range(x.shape[1] // 256):
        sq = jnp.square(x[:, j * 256:(j + 1) * 256])
        hi = sq.astype(BF16)
        lo = (sq - hi.astype(F32)).astype(BF16)
        outs.append(jnp.dot(hi, ones_bd, preferred_element_type=F32) + jnp.dot(lo, ones_bd, preferred_element_type=F32))
    return jnp.concatenate(outs, axis=-1) * (1.0 / HEAD_DIM)


def _feat_base(h):
    return FEAT * (h // 2) + HEAD_DIM * (1 - h % 2)


def _bias_features(c, n_heads):
    terms = jnp.concatenate(_split3(c * LOG2E), axis=1)
    r = lax.broadcasted_iota(jnp.int32, (3 * LANES, LANES), 0)
    l = lax.broadcasted_iota(jnp.int32, (3 * LANES, LANES), 1)
    head = r % LANES
    off = l - _feat_base(head)
    in_head = (off >= 0) & (off < FEAT) & (head < n_heads)
    place_q = (in_head & (off == r // LANES)).astype(BF16)
    place_k = (in_head & (off == 3 + r // LANES)).astype(BF16)
    lane = lax.broadcasted_iota(jnp.int32, (1, LANES), 1)
    slot = lane % FEAT
    used = ((lane % HEAD_DIM) // FEAT) < (n_heads // 2)
    ones_q = (used & (slot >= 3) & (slot < 6)).astype(F32)
    ones_k = (used & (slot < 3)).astype(F32)
    qf = jnp.dot(terms, place_q, preferred_element_type=F32) + ones_q
    kf = ones_k - jnp.dot(terms, place_k, preferred_element_type=F32)
    return qf.astype(BF16), kf.astype(BF16)


def _fox_in_kernel(h_ref, gain_ref, w_ref, bf_ref, qkg_ref, q_ref, k_ref, v_ref, kb_ref, vt_ref, lf_ref, c_ref,
                   qf_ref, kf_ref, og_ref, cq_ref, carry_ref, *, sw, n_heads, shift, steps, blocks_per_seq):
    xn = _rms(h_ref[...], gain_ref[...]).astype(BF16)
    tm = xn.shape[0]
    ri = lax.broadcasted_iota(jnp.int32, (256, 256), 0) // HEAD_DIM
    ci = lax.broadcasted_iota(jnp.int32, (256, 256), 1) // HEAD_DIM
    ones_bd = (ri == ci).astype(BF16)
    q = jnp.dot(xn, w_ref[:, 0:sw], preferred_element_type=F32)
    q = q * lax.rsqrt(_head_mean_sq(q, ones_bd) + NORM_EPS) * qkg_ref[0:1]
    q_ref[...] = (q * (ATTN_SCALE * LOG2E)).astype(BF16)
    k = jnp.dot(xn, w_ref[:, sw:2 * sw], preferred_element_type=F32)
    k = k * lax.rsqrt(_head_mean_sq(k, ones_bd) + NORM_EPS) * qkg_ref[1:2]
    k_ref[...] = k
    kb_ref[...] = k.astype(BF16)
    v = jnp.dot(xn, w_ref[:, 2 * sw:3 * sw], preferred_element_type=F32)
    v_ref[...] = v
    vt_ref[...] = v.T.astype(BF16)
    o0 = 3 * sw + LANES
    lf = -_softplus(-(jnp.dot(xn, w_ref[:, 3 * sw:o0], preferred_element_type=F32) + bf_ref[...]))
    lf_ref[...] = lf
    og_ref[...] = _sigmoid(jnp.dot(xn, w_ref[:, o0:o0 + sw], preferred_element_type=F32))
    cq_ref[...] = jnp.dot(xn, w_ref[:, o0 + sw:], preferred_element_type=F32).astype(BF16)
    if steps is None:
        @pl.when(pl.program_id(0) % blocks_per_seq == 0)
        def _():
            carry_ref[...] = jnp.zeros_like(carry_ref)

        c = _cumsum_chunks(lf, tm) + carry_ref[0:1]
        carry_ref[...] = jnp.broadcast_to(c[tm - 1:tm], carry_ref.shape)
    else:
        parts = [lf[0:shift]]
        for t in range(1, steps):
            parts.append(parts[-1] + lf[t * shift:(t + 1) * shift])
        c = jnp.concatenate(parts, axis=0)
    c_ref[...] = c
    qf, kf = _bias_features(c, n_heads)
    qf_ref[...] = qf
    kf_ref[...] = kf


def _fox_in(h, gain, w_in, bf, qk_gain, *, tm, shift, steps, blocks_per_seq):
    n, d = h.shape
    n_heads = bf.shape[0]
    sw = n_heads * HEAD_DIM
    w = jnp.concatenate([w_in[:, :3 * sw], _pad_cols(w_in[:, 3 * sw:3 * sw + n_heads], LANES),
                         w_in[:, 3 * sw + n_heads:]], axis=1).astype(BF16)
    qkg = _pad_rows(jnp.tile(qk_gain, (1, n_heads)), 8)
    row = lambda i: (i, 0)
    wide = pl.BlockSpec((tm, sw), row)
    lane = pl.BlockSpec((tm, LANES), row)
    wide_f32 = jax.ShapeDtypeStruct((n, sw), F32)
    wide_bf16 = jax.ShapeDtypeStruct((n, sw), BF16)
    lane_f32 = jax.ShapeDtypeStruct((n, LANES), F32)
    lane_bf16 = jax.ShapeDtypeStruct((n, LANES), BF16)
    return pl.pallas_call(
        functools.partial(_fox_in_kernel, sw=sw, n_heads=n_heads, shift=shift, steps=steps,
                          blocks_per_seq=blocks_per_seq),
        grid=(n // tm,),
        in_specs=[pl.BlockSpec((tm, d), row), _const_spec((1, d)), _const_spec(w.shape), _const_spec((1, LANES)),
                  _const_spec((8, sw))],
        out_specs=[wide, wide, wide, wide, pl.BlockSpec((sw, tm), lambda i: (0, i)), lane, lane, lane, lane, wide,
                   pl.BlockSpec((tm, MEM_WIDTH), row)],
        out_shape=[wide_bf16, wide_f32, wide_f32, wide_bf16, jax.ShapeDtypeStruct((sw, n), BF16), lane_f32, lane_f32,
                   lane_bf16, lane_bf16,
                   wide_f32, jax.ShapeDtypeStruct((n, MEM_WIDTH), BF16)],
        scratch_shapes=[pltpu.VMEM((8, LANES), F32)],
        compiler_params=_params(1),
        name="fox_in",
    )(h, gain.reshape(1, d), w, _pad_cols(bf.reshape(1, n_heads), LANES), qkg)


KEY_TILE = 128


def _fox_flash_kernel(q_ref, k_ref, vt_ref, qf_ref, kf_ref, og_ref, o_ref, m_ref, l_ref, acc_ref, *, n_heads, tq, tk):
    qi = pl.program_id(1)
    ki = pl.program_id(2)
    last = ((qi + 1) * tq - 1) // tk
    hd = HEAD_DIM
    hs = range(n_heads)

    @pl.when(ki == 0)
    def _():
        m_ref[...] = jnp.full_like(m_ref, NEG_BIG)
        l_ref[...] = jnp.zeros_like(l_ref)
        acc_ref[...] = jnp.zeros_like(acc_ref)

    lane = lax.broadcasted_iota(jnp.int32, (1, LANES), 1)
    upper = lane >= hd
    zero = jnp.zeros((), BF16)

    def step(masked):
        qf = qf_ref[...]
        kf = kf_ref[...]
        if masked:
            kpos = ki * tk + lax.broadcasted_iota(jnp.int32, (KEY_TILE, tq), 0)
            qpos = qi * tq + lax.broadcasted_iota(jnp.int32, (KEY_TILE, tq), 1)
        q2 = [q_ref[:, (h // 2) * LANES:(h // 2 + 1) * LANES] for h in hs]
        k2 = [k_ref[:, (h // 2) * LANES:(h // 2 + 1) * LANES] for h in hs]
        own = [upper == bool(h % 2) for h in hs]
        feat = [(lane >= _feat_base(h)) & (lane < _feat_base(h) + FEAT) for h in hs]
        q_aug = [jnp.where(own[h], q2[h], jnp.where(feat[h], qf, zero)) for h in hs]
        k_aug = [jnp.where(own[h], k2[h], kf) for h in hs]
        m = [m_ref[h:h + 1, :] for h in hs]
        l = [l_ref[h:h + 1, :] for h in hs]
        acc = [acc_ref[h * hd:(h + 1) * hd, :] for h in hs]
        for t in range(tk // KEY_TILE):
            rows = slice(t * KEY_TILE, (t + 1) * KEY_TILE)
            s = [_dot_nt(k_aug[h][rows], q_aug[h]) for h in hs]
            if masked:
                visible = (kpos + t * KEY_TILE) <= qpos
                s = [jnp.where(visible, x, NEG_BIG) for x in s]
            m_new = [jnp.maximum(m[h], jnp.max(s[h], axis=0, keepdims=True)) for h in hs]
            alpha = [jnp.exp2(m[h] - m_new[h]) for h in hs]
            p = [jnp.exp2(s[h] - m_new[h]) for h in hs]
            l = [alpha[h] * l[h] + jnp.sum(p[h], axis=0, keepdims=True) for h in hs]
            pv = [jnp.dot(vt_ref[h * hd:(h + 1) * hd, rows], p[h].astype(BF16), preferred_element_type=F32)
                  for h in hs]
            acc = [alpha[h] * acc[h] + pv[h] for h in hs]
            m = m_new
        for h in hs:
            m_ref[h:h + 1, :] = m[h]
            l_ref[h:h + 1, :] = l[h]
            acc_ref[h * hd:(h + 1) * hd, :] = acc[h]

    @pl.when(ki < last)
    def _():
        step(False)

    @pl.when(ki == last)
    def _():
        step(True)
        parts = [acc_ref[h * hd:(h + 1) * hd, :] * (1.0 / l_ref[h:h + 1, :]) for h in hs]
        o_ref[...] = jnp.concatenate(parts, axis=0).T * og_ref[...]


def _fox_flash(q, k, vt, qf, kf, og, *, n_seq, seq_len, tq, tk):
    n, sw = k.shape
    n_heads = sw // HEAD_DIM
    nq, nk = seq_len // tq, seq_len // tk
    kblk = lambda i, j: jnp.minimum(j, ((i + 1) * tq - 1) // tk)
    qrow = lambda b, i, j: (b * nq + i, 0)
    krow = lambda b, i, j: (b * nk + kblk(i, j), 0)
    return pl.pallas_call(
        functools.partial(_fox_flash_kernel, n_heads=n_heads, tq=tq, tk=tk),
        grid=(n_seq, nq, nk),
        in_specs=[pl.BlockSpec((tq, sw), qrow), pl.BlockSpec((tk, sw), krow),
                  pl.BlockSpec((sw, tk), lambda b, i, j: (0, b * nk + kblk(i, j))),
                  pl.BlockSpec((tq, LANES), qrow), pl.BlockSpec((tk, LANES), krow), pl.BlockSpec((tq, sw), qrow)],
        out_specs=pl.BlockSpec((tq, sw), qrow),
        out_shape=jax.ShapeDtypeStruct((n, sw), F32),
        scratch_shapes=[pltpu.VMEM((HEAD_ROWS, tq), F32), pltpu.VMEM((HEAD_ROWS, tq), F32),
                        pltpu.VMEM((sw, tq), F32)],
        compiler_params=_params(3),
        name="fox_flash",
    )(q, k, vt, qf, kf, og)


HEAD_ROWS = 16
PAGE_GROUP = 8


def _fox_paged_kernel(pt_ref, q_ref, cn_ref, kn_ref, vn_ref, og_ref, k_hbm, v_hbm, lp_hbm, o_ref,
                      kbuf, vbuf, lpbuf, sem, m_ref, l_ref, acc_ref, suf_ref, *, layer, n_heads, n_pages, group, page):
    i = pl.program_id(0)
    j = pl.program_id(1)
    n_groups = pl.num_programs(1)
    g = i * n_groups + j
    hd = HEAD_DIM
    hs = range(n_heads)

    def copies(seq, grp, slot):
        out = []
        for u in range(group):
            pg = pt_ref[seq, n_pages - 1 - (grp * group + u)]
            out.append(pltpu.make_async_copy(k_hbm.at[layer, pg], kbuf.at[slot, u], sem.at[slot, 0]))
            out.append(pltpu.make_async_copy(v_hbm.at[layer, pg], vbuf.at[slot, u], sem.at[slot, 1]))
            out.append(pltpu.make_async_copy(lp_hbm.at[layer, pg], lpbuf.at[slot, u, pl.ds(0, n_heads)], sem.at[slot, 2]))
        return out

    @pl.when(g == 0)
    def _():
        lpbuf[:, :, n_heads:, :] = jnp.zeros((2, group, HEAD_ROWS - n_heads, page), F32)
        for cp in copies(0, 0, 0):
            cp.start()

    @pl.when(g + 1 < pl.num_programs(0) * n_groups)
    def _():
        wrap = j + 1 == n_groups
        for cp in copies(jnp.where(wrap, i + 1, i), jnp.where(wrap, 0, j + 1), (g + 1) % 2):
            cp.start()

    slot = g % 2
    q = q_ref[0]
    q_ = [q[:, h * hd:(h + 1) * hd] for h in hs]

    @pl.when(j == 0)
    def _():
        kn, vn = kn_ref[0], vn_ref[0]
        steps = q.shape[0]
        trow = lax.broadcasted_iota(jnp.int32, (steps, steps), 0)
        tcol = lax.broadcasted_iota(jnp.int32, (steps, steps), 1)
        s = [jnp.where(tcol <= trow,
                       _dot_nt(q_[h], kn[:, h * hd:(h + 1) * hd]) + LOG2E * cn_ref[0, h][:, 0:steps], NEG_BIG)
             for h in hs]
        m = [jnp.max(x, axis=-1, keepdims=True) for x in s]
        p = [jnp.exp2(s[h] - m[h]) for h in hs]
        for h in hs:
            m_ref[h] = m[h]
            l_ref[h] = jnp.sum(p[h], axis=-1, keepdims=True)
            acc_ref[h] = _dot(p[h], vn[:, h * hd:(h + 1) * hd])
        suf_ref[...] = jnp.zeros_like(suf_ref)

    for cp in copies(i, j, slot):
        cp.wait()

    lp = [lpbuf[slot, u] for u in range(group)]
    ri = lax.broadcasted_iota(jnp.int32, (page, page), 0)
    ci = lax.broadcasted_iota(jnp.int32, (page, page), 1)
    later = (ri > ci).astype(BF16)
    hi, mid, lo = _split3(jnp.concatenate(lp, axis=0))
    within = (jnp.dot(hi, later, preferred_element_type=F32) + jnp.dot(mid, later, preferred_element_type=F32)
              + jnp.dot(lo, later, preferred_element_type=F32))
    carry = [suf_ref[...]]
    for u in range(group):
        carry.append(carry[-1] + jnp.sum(lp[u], axis=-1, keepdims=True))
    suf_ref[...] = carry[group]
    n_self = q.shape[0]
    bias = [jnp.concatenate([within[u * HEAD_ROWS + h:u * HEAD_ROWS + h + 1] + carry[u][h:h + 1]
                             for u in range(group)], axis=1) for h in hs]
    kcat = [jnp.concatenate([kbuf[slot, u, h] for u in range(group)], axis=1).astype(BF16) for h in hs]
    vcat = [jnp.concatenate([vbuf[slot, u, h] for u in range(group)], axis=1).astype(BF16) for h in hs]
    s = [_dot(q_[h], kcat[h]) + LOG2E * (bias[h] + cn_ref[0, h][:, n_self:n_self + 1]) for h in hs]
    m_old = [m_ref[h] for h in hs]
    m_new = [jnp.maximum(m_old[h], jnp.max(s[h], axis=-1, keepdims=True)) for h in hs]
    alpha = [jnp.exp2(m_old[h] - m_new[h]) for h in hs]
    p = [jnp.exp2(s[h] - m_new[h]) for h in hs]
    pv = [_dot_nt(p[h], vcat[h]) for h in hs]
    for h in hs:
        l_ref[h] = alpha[h] * l_ref[h] + jnp.sum(p[h], axis=-1, keepdims=True)
        acc_ref[h] = alpha[h] * acc_ref[h] + pv[h]
        m_ref[h] = m_new[h]

    @pl.when(j == n_groups - 1)
    def _():
        o_ref[0] = jnp.concatenate([acc_ref[h] / l_ref[h] for h in hs], axis=-1) * og_ref[0]


def _fox_paged(page_table, q, cn, k_new, v_new, og, cache_k, cache_v, cache_lp_t, *, layer, group):
    b, steps, sw = q.shape
    n_heads = sw // HEAD_DIM
    n_pages = page_table.shape[1]
    page = cache_k.shape[4]
    seq3 = lambda i, j, pt: (i, 0, 0)
    seq4 = lambda i, j, pt: (i, 0, 0, 0)
    tok = pl.BlockSpec((1, steps, sw), seq3)
    hbm = pl.BlockSpec(memory_space=pl.ANY)
    grid_spec = pltpu.PrefetchScalarGridSpec(
        num_scalar_prefetch=1,
        grid=(b, n_pages // group),
        in_specs=[tok, pl.BlockSpec((1, n_heads, steps, 2 * steps), seq4), tok, tok, tok, hbm, hbm, hbm],
        out_specs=tok,
        scratch_shapes=[pltpu.VMEM((2, group, n_heads, HEAD_DIM, page), F32),
                        pltpu.VMEM((2, group, n_heads, HEAD_DIM, page), F32),
                        pltpu.VMEM((2, group, HEAD_ROWS, page), F32),
                        pltpu.SemaphoreType.DMA((2, 3)),
                        pltpu.VMEM((n_heads, steps, 1), F32), pltpu.VMEM((n_heads, steps, 1), F32),
                        pltpu.VMEM((n_heads, steps, HEAD_DIM), F32), pltpu.VMEM((HEAD_ROWS, 1), F32)],
    )
    return pl.pallas_call(
        functools.partial(_fox_paged_kernel, layer=layer, n_heads=n_heads, n_pages=n_pages, group=group, page=page),
        grid_spec=grid_spec,
        out_shape=jax.ShapeDtypeStruct((b, steps, sw), F32),
        compiler_params=_params(2),
        name="fox_paged",
    )(page_table, q, cn, k_new, v_new, og, cache_k, cache_v, cache_lp_t)


def _to_time_major(x):
    x = jnp.swapaxes(x, 0, 1)
    return x.reshape((x.shape[0] * x.shape[1],) + x.shape[2:])


def _to_batch_major(x, b):
    return jnp.swapaxes(x.reshape((x.shape[0] // b, b) + x.shape[1:]), 0, 1)


def _trunk(h, P, *, prompt, n_seq, seq_len, mem_k, mem_v, rw_shift, rw_wkv, ffn_conv, fox_cache):
    depth = P["norm_gains"].shape[0]
    n, d = h.shape
    sw = P["rw_vecs"].shape[2]
    n_heads = sw // HEAD_DIM
    if prompt:
        tm, shift, blocks_per_seq = 256, 1, seq_len // 256
    else:
        tm, shift, blocks_per_seq = n, n_seq, 1
    tm_ffn = min(512, seq_len) if prompt else n
    v_first = None
    shifts, wkvs, fks, fvs, fls, convs = [], [], [], [], [], []
    for l in range(depth):
        gains = P["norm_gains"][l]
        i = l // 2
        if l % 2 == 0:
            init = jnp.zeros((n_seq, 1, d), F32) if prompt else rw_shift[i][None]
            vj = None if i == 0 else i - 1
            r, lw, k, v, kk, a, g, cq, sh = _rwkv_in(
                h, init, v_first, gains[0], P["rw_mix"][i], P["rw_w_in"][i], P["rw_w1"][i], P["rw_w2"][i],
                P["rw_a1"][i], P["rw_a2"][i], P["rw_g1"][i], P["rw_g2"][i],
                None if vj is None else P["rw_v1"][vj], None if vj is None else P["rw_v2"][vj],
                None if vj is None else P["rw_v0"][vj], P["rw_vecs"][i],
                tm=tm, shift=shift, blocks_per_seq=blocks_per_seq)
            if v_first is None:
                v_first = v
            if prompt:
                o_self, s_fin = _wkv(r, lw, k, v, kk, a, g, P["rw_vecs"][i], None, n_seq=n_seq, seq_len=seq_len,
                                     chunk=WKV_CHUNK, n_chunks=4)
                shifts.append(sh[:, 0])
            else:
                pad = lambda x: jnp.pad(_to_batch_major(x, n_seq), ((0, 0), (0, 8 - seq_len), (0, 0))).reshape(n_seq * 8, sw)
                o_pad, s_fin = _wkv(*[pad(x) for x in (r, lw, k, v, kk, a, g)], P["rw_vecs"][i], rw_wkv[i],
                                    n_seq=n_seq, seq_len=8, chunk=8, n_chunks=1)
                o_self = _to_time_major(o_pad.reshape(n_seq, 8, sw)[:, :seq_len])
                shifts.append(sh[0])
            wkvs.append(s_fin)
        else:
            q, k, v, kb, vt, lf, c, qf, kf, og, cq = _fox_in(
                h, gains[0], P["fox_w_in"][i], P["fox_bf"][i], P["fox_qk_gain"][i],
                tm=tm, shift=shift, steps=None if prompt else seq_len, blocks_per_seq=blocks_per_seq)
            if prompt:
                o_self = _fox_flash(q, kb, vt, qf, kf, og, n_seq=n_seq, seq_len=seq_len, tq=256,
                                    tk=min(512, seq_len))
            else:
                cache_k, cache_v, cache_lp_t, page_table = fox_cache
                pad8 = lambda x: jnp.pad(_to_batch_major(x, n_seq), ((0, 0), (0, 8 - seq_len), (0, 0)))
                ch = jnp.swapaxes(pad8(c)[:, :, :n_heads], 1, 2)
                cn = jnp.concatenate([ch[:, :, :, None] - ch[:, :, None, :],
                                      jnp.broadcast_to(ch[..., None], ch.shape + (8,))], axis=-1)
                o_b = _fox_paged(page_table, pad8(q), cn, pad8(k), pad8(v), pad8(og), cache_k, cache_v, cache_lp_t,
                                 layer=i, group=min(PAGE_GROUP, page_table.shape[1]))
                o_self = _to_time_major(o_b[:, :seq_len])
            fks.append(k)
            fvs.append(v)
            fls.append(lf[:, :n_heads])
        if prompt:
            o_mem = _mem_attn(cq.reshape(n_seq, seq_len, MEM_WIDTH), mem_k[l], mem_v[l],
                              tq=min(1024, seq_len)).reshape(n, MEM_WIDTH)
        else:
            cq_b = jnp.pad(_to_batch_major(cq, n_seq), ((0, 0), (0, 16 - seq_len), (0, 0)))
            o_mem = _to_time_major(_mem_attn(cq_b, mem_k[l], mem_v[l], tq=16)[:, :seq_len])
        h = _out_proj(o_self, o_mem, h, P["w_out"][l], gains[1], tm=tm)
        if prompt:
            conv_init = jnp.zeros((n_seq, CONV_W - 1, P["ffn_conv_b"].shape[1]), F32)
        else:
            conv_init = _to_time_major(ffn_conv[l])[None]
        h, cs = _ffn(h, conv_init, gains[2], gains[3], P["ffn_w_up"][l], P["ffn_conv_w"][l], P["ffn_conv_b"][l],
                     P["ffn_w_down"][l], tm=tm_ffn, shift=shift, blocks_per_seq=seq_len // tm_ffn if prompt else 1)
        convs.append(cs if prompt else _to_batch_major(cs[0], n_seq))
    return h, wkvs, shifts, fks, fvs, fls, convs


def kernel(x_prompt, x_sample, cache_mem_k, cache_mem_v, cache_fox_k, cache_fox_v, cache_fox_logf, state_rwkv_wkv,
           state_rwkv_shift, state_ffn_conv, page_table, mem_prompt, norm_gains, w_out, mem_norm, w_mem_kv, rw_mix,
           rw_w_in, rw_vecs, rw_w1, rw_w2, rw_a1, rw_a2, rw_g1, rw_g2, rw_v0, rw_v1, rw_v2, fox_w_in, fox_bf,
           fox_qk_gain, ffn_w_up, ffn_conv_w, ffn_conv_b, ffn_w_down):
    P = dict(norm_gains=norm_gains, w_out=w_out, rw_mix=rw_mix, rw_w_in=rw_w_in, rw_vecs=rw_vecs, rw_w1=rw_w1,
             rw_w2=rw_w2, rw_a1=rw_a1, rw_a2=rw_a2, rw_g1=rw_g1, rw_g2=rw_g2, rw_v0=rw_v0, rw_v1=rw_v1, rw_v2=rw_v2,
             fox_w_in=fox_w_in, fox_bf=fox_bf, fox_qk_gain=fox_qk_gain, ffn_w_up=ffn_w_up, ffn_conv_w=ffn_conv_w,
             ffn_conv_b=ffn_conv_b, ffn_w_down=ffn_w_down)
    b, s, d = x_prompt.shape
    db, ds, _ = x_sample.shape
    depth = norm_gains.shape[0]
    n_heads = fox_bf.shape[1]
    sw = n_heads * HEAD_DIM

    p_mem_k, p_mem_v = _mem_kv(mem_prompt, mem_norm, w_mem_kv)
    hp, p_wkv, p_shift, p_fk, p_fv, p_fl, p_conv = _trunk(
        x_prompt.reshape(b * s, d), P, prompt=True, n_seq=b, seq_len=s, mem_k=p_mem_k, mem_v=p_mem_v,
        rw_shift=None, rw_wkv=None, ffn_conv=None, fox_cache=None)

    cache_k = jnp.transpose(cache_fox_k, (0, 1, 3, 4, 2))
    cache_v = jnp.transpose(cache_fox_v, (0, 1, 3, 4, 2))
    cache_lp_t = jnp.swapaxes(cache_fox_logf, 2, 3)
    hs, s_wkv, s_shift, s_fk, s_fv, s_fl, s_conv = _trunk(
        _to_time_major(x_sample), P, prompt=False, n_seq=db, seq_len=ds,
        mem_k=cache_mem_k.reshape(depth, db, N_MEM, MEM_WIDTH), mem_v=cache_mem_v.reshape(depth, db, N_MEM, MEM_WIDTH),
        rw_shift=state_rwkv_shift, rw_wkv=state_rwkv_wkv, ffn_conv=state_ffn_conv,
        fox_cache=(cache_k, cache_v, cache_lp_t, page_table))

    heads = lambda x, bb, tt: x.reshape(bb, tt, n_heads, HEAD_DIM)
    mem_heads = lambda x: x.reshape(depth, b, N_MEM, H_MEM, HEAD_DIM)
    return (hp.reshape(b, s, d), _to_batch_major(hs, db),
            jnp.stack(p_wkv), jnp.stack(p_shift),
            jnp.stack([heads(x, b, s) for x in p_fk]), jnp.stack([heads(x, b, s) for x in p_fv]),
            jnp.stack([x.reshape(b, s, n_heads) for x in p_fl]),
            mem_heads(p_mem_k), mem_heads(p_mem_v), jnp.stack(p_conv),
            jnp.stack(s_wkv), jnp.stack(s_shift),
            jnp.stack([heads(_to_batch_major(x, db), db, ds) for x in s_fk]),
            jnp.stack([heads(_to_batch_major(x, db), db, ds) for x in s_fv]),
            jnp.stack([_to_batch_major(x, db) for x in s_fl]),
            jnp.stack(s_conv))
```
